```python
import math, functools
import jax, jax.numpy as jnp
from jax import lax
import numpy as np

D_MODEL = 1024
BATCH = 8
SEQ = 4096
DEPTH = 2
DEC_BATCH = 32
DEC_SEQ = 8
PAST_LEN = 16384
PAGE_SIZE = 128

C_CONV = D_MODEL
CONV_WIDTH = 31
N_RWKV = 64
H_RWKV = D_MODEL // N_RWKV
C_RWKV = H_RWKV * N_RWKV
LORA_DECAY = 64
LORA_ICLR = 64
LORA_GATE = 128
RWKV_GN_EPS = 64e-5
DK_DIFF = 64
DV_DIFF = 2 * DK_DIFF
H_DIFF = D_MODEL // DV_DIFF
C_DIFF = H_DIFF * DV_DIFF
DIFF_SCALE = DK_DIFF ** -0.5
Q_BLOCK = 128
N_MEM = 256
H_MEM = 4
DH_MEM = D_MODEL // H_MEM
N_KEYS = 128
N_EXPERTS = N_KEYS * N_KEYS
H_PEER = 8
D_PEER_Q = 256
D_HALF = D_PEER_Q // 2
TOPK_HALF = 16
TOPK_PEER = 16
PEER_BLOCK = 128
N_BRANCH = 3
W_RWKV_IN = 3 * C_RWKV + LORA_DECAY + LORA_ICLR + LORA_GATE
OFF_RWKV = 2 * C_CONV
OFF_DIFF = OFF_RWKV + W_RWKV_IN
OFF_GATE = OFF_DIFF + 3 * C_DIFF
D_IN_TOTAL = OFF_GATE + N_BRANCH * D_MODEL
EPS = 1e-6

kernel_name = 'hybrid_conv_rwkv7_diffattn_peer_step'


def rmsnorm(x, g, eps=EPS):
    xf = x.astype(jnp.float32)
    y = xf * lax.rsqrt(jnp.mean(xf * xf, -1, keepdims=True) + eps)
    return (y * g.astype(jnp.float32)).astype(x.dtype)


def layernorm(x, g, b, eps=1e-5):
    xf = x.astype(jnp.float32)
    mu = jnp.mean(xf, -1, keepdims=True)
    var = jnp.mean(jnp.square(xf - mu), -1, keepdims=True)
    return ((xf - mu) * lax.rsqrt(var + eps) * g.astype(jnp.float32) + b.astype(jnp.float32)).astype(x.dtype)


def conv_branch(z_conv, conv_buf, lp):
    a, b = jnp.split(z_conv, 2, axis=-1)
    u = a * jax.nn.sigmoid(b)
    ext = jnp.concatenate([conv_buf.astype(u.dtype), u], axis=1)
    y = lax.conv_general_dilated(ext, lp['conv_w'].astype(u.dtype)[:, None, :], (1,), 'VALID',
                                 dimension_numbers=('NWC', 'WIO', 'NWC'), feature_group_count=C_CONV)
    y = jax.nn.silu(layernorm(y + lp['conv_b'], lp['conv_ln_g'], lp['conv_ln_b']))
    return y @ lp['w_conv_out'], ext[:, -(CONV_WIDTH - 1):]


def rwkv_branch(z_r, z_prev, wkv0, lp):
    f32 = jnp.float32
    B, T, _ = z_r.shape
    z_shift = jnp.concatenate([z_prev.astype(z_r.dtype), z_r[:, :-1]], axis=1)
    zs = z_r + (z_shift - z_r) * lp['rwkv_mu']
    c = C_RWKV
    r, k, v = zs[..., :c], zs[..., c:2 * c], zs[..., 2 * c:3 * c]
    wd = zs[..., 3 * c:3 * c + LORA_DECAY]
    ad = zs[..., 3 * c + LORA_DECAY:3 * c + LORA_DECAY + LORA_ICLR]
    gd = zs[..., 3 * c + LORA_DECAY + LORA_ICLR:]
    w = -jax.nn.softplus(-(lp['rwkv_w0'] + jnp.tanh(wd) @ lp['rwkv_w2']).astype(f32)) - 0.5
    decay = jnp.exp(-jnp.exp(w))
    a = jax.nn.sigmoid((lp['rwkv_a0'] + ad @ lp['rwkv_a2']).astype(f32))
    g = (jax.nn.sigmoid(gd) @ lp['rwkv_g2']).astype(f32)
    heads = lambda t: t.astype(f32).reshape(B, T, H_RWKV, N_RWKV)
    kk = heads(k * lp['rwkv_kk'])
    kk = kk * lax.rsqrt(jnp.maximum(jnp.sum(kk * kk, -1, keepdims=True), 1e-24))
    k_mod = k.astype(f32) * (1.0 + (a - 1.0) * lp['rwkv_ka'].astype(f32))
    r_h, k_h, v_h, w_h, a_h = heads(r), heads(k_mod), heads(v), heads(decay), heads(a)
    b_h = kk * a_h

    def step(S, inp):
        r_t, w_t, k_t, v_t, kk_t, b_t = inp
        S = (S * w_t[:, :, None, :]
             - jnp.einsum('bhvk,bhk->bhv', S, kk_t)[..., None] * b_t[:, :, None, :]
             + v_t[..., None] * k_t[:, :, None, :])
        return S, jnp.einsum('bhvk,bhk->bhv', S, r_t)

    tm = lambda t: jnp.moveaxis(t, 1, 0)
    S_fin, y = lax.scan(step, wkv0.astype(f32), (tm(r_h), tm(w_h), tm(k_h), tm(v_h), tm(kk), tm(b_h)))
    y = jnp.moveaxis(y, 0, 1)
    mu = jnp.mean(y, -1, keepdims=True)
    var = jnp.mean(jnp.square(y - mu), -1, keepdims=True)
    y = ((y - mu) * lax.rsqrt(var + RWKV_GN_EPS)).reshape(B, T, C_RWKV) * lp['rwkv_gn_g'] + lp['rwkv_gn_b']
    bonus = jnp.sum(r_h * k_h * lp['rwkv_rk'].astype(f32), -1, keepdims=True) * v_h
    y = (y + bonus.reshape(B, T, C_RWKV)) * g
    return y.astype(z_r.dtype) @ lp['w_rwkv_out'], S_fin


def diff_lambda(lq1, lk1, lq2, lk2, lam_init):
    f32 = jnp.float32
    return (jnp.exp(jnp.sum(lq1.astype(f32) * lk1.astype(f32)))
            - jnp.exp(jnp.sum(lq2.astype(f32) * lk2.astype(f32))) + lam_init)


def diff_attn_core(q, k, v, mask, lam):
    s = jnp.einsum('bthmd,bshmd->bmhts', q, k).astype(jnp.float32) * DIFF_SCALE
    s = jnp.where(mask, s, -jnp.inf)
    p = jax.nn.softmax(s, axis=-1)
    a = p[:, 0] - lam * p[:, 1]
    return jnp.einsum('bhts,bshd->bthd', a, v.astype(jnp.float32)).astype(q.dtype)


def diff_attn_prompt(q, k, v, lam):
    B, S = q.shape[:2]
    n_qb = S // Q_BLOCK
    qb = jnp.moveaxis(q.reshape(B, n_qb, Q_BLOCK, H_DIFF, 2, DK_DIFF), 1, 0)
    kpos = jnp.arange(S)

    def blk(args):
        q_blk, start = args
        qpos = start + jnp.arange(Q_BLOCK)
        return diff_attn_core(q_blk, k, v, kpos[None, :] <= qpos[:, None], lam)

    o = lax.map(blk, (qb, jnp.arange(n_qb) * Q_BLOCK))
    return jnp.moveaxis(o, 0, 1).reshape(B, S, H_DIFF, DV_DIFF)


def diff_attn_sample(q, k, v, lam, cache_k, cache_v, page_table, layer):
    Tn = q.shape[1]
    past = page_table.shape[1] * PAGE_SIZE
    mask = jnp.concatenate([jnp.ones((Tn, past), bool), jnp.tril(jnp.ones((Tn, Tn), bool))], axis=1)

    def one(args):
        q_s, k_s, v_s, pages = args
        kp = cache_k[layer, pages].reshape(past, H_DIFF, 2, DK_DIFF).astype(k_s.dtype)
        vp = cache_v[layer, pages].reshape(past, H_DIFF, DV_DIFF).astype(v_s.dtype)
        kc = jnp.concatenate([kp, k_s], axis=0)[None]
        vc = jnp.concatenate([vp, v_s], axis=0)[None]
        return diff_attn_core(q_s[None], kc, vc, mask, lam)[0]

    return lax.map(one, (q, k, v, page_table))


def hybrid_mix(h, z_prev_rwkv, conv_buf, wkv0, attend, lp, lam, lam_init):
    B, T, _ = h.shape
    z = h @ lp['w_in']
    gates = jax.nn.sigmoid(z[..., OFF_GATE:].astype(jnp.float32)).reshape(B, T, N_BRANCH, D_MODEL)
    y_conv, conv_new = conv_branch(z[..., :OFF_RWKV], conv_buf, lp)
    y_rwkv, wkv_new = rwkv_branch(z[..., OFF_RWKV:OFF_DIFF], z_prev_rwkv, wkv0, lp)
    zd = z[..., OFF_DIFF:OFF_GATE]
    q = zd[..., :C_DIFF].reshape(B, T, H_DIFF, 2, DK_DIFF)
    k = zd[..., C_DIFF:2 * C_DIFF].reshape(B, T, H_DIFF, 2, DK_DIFF)
    v = zd[..., 2 * C_DIFF:].reshape(B, T, H_DIFF, DV_DIFF)
    o = attend(q, k, v, lam)
    o = rmsnorm(o, lp['attn_subln'], 1e-5) * (1.0 - lam_init)
    y_diff = o.reshape(B, T, C_DIFF) @ lp['w_attn_out']
    merged = gates[:, :, 0] * y_conv + gates[:, :, 1] * y_rwkv + gates[:, :, 2] * y_diff
    out = merged.astype(h.dtype) @ lp['w_o']
    return out, conv_new, wkv_new, k.reshape(B, T, H_DIFF, 2 * DK_DIFF), v


def memory_kv(mem, norm_mem, w_mk, w_mv):
    B = mem.shape[0]
    hm = rmsnorm(mem, norm_mem)
    return ((hm @ w_mk).reshape(B, N_MEM, H_MEM, DH_MEM), (hm @ w_mv).reshape(B, N_MEM, H_MEM, DH_MEM))


def cross_attn(x, mk, mv, norm_cross, w_cq, w_co):
    B, T, _ = x.shape
    q = (rmsnorm(x, norm_cross) @ w_cq).reshape(B, T, H_MEM, DH_MEM)
    s = jnp.einsum('bthd,bmhd->bhtm', q, mk.astype(q.dtype)).astype(jnp.float32) * (DH_MEM ** -0.5)
    p = jax.nn.softmax(s, axis=-1)
    o = jnp.einsum('bhtm,bmhd->bthd', p, mv.astype(jnp.float32)).astype(x.dtype)
    return o.reshape(B, T, D_MODEL) @ w_co


def peer_ffn(x, norm_g, w_pq, k1, k2, u_tab, v_tab):
    B, T, D = x.shape
    n = B * T
    h = rmsnorm(x, norm_g).reshape(n, D)
    pad = (-n) % PEER_BLOCK
    hb_all = jnp.pad(h, ((0, pad), (0, 0))).reshape(-1, PEER_BLOCK, D)
    f32 = jnp.float32

    def blk(hb):
        q = (hb @ w_pq).reshape(PEER_BLOCK, H_PEER, 2, D_HALF).astype(f32)
        s1 = jnp.einsum('thd,nd->thn', q[:, :, 0], k1.astype(f32))
        s2 = jnp.einsum('thd,nd->thn', q[:, :, 1], k2.astype(f32))
        v1, i1 = lax.top_k(s1, TOPK_HALF)
        v2, i2 = lax.top_k(s2, TOPK_HALF)
        cand = (v1[..., :, None] + v2[..., None, :]).reshape(PEER_BLOCK, H_PEER, TOPK_HALF * TOPK_HALF)
        sc, ci = lax.top_k(cand, TOPK_PEER)
        e = (jnp.take_along_axis(i1, ci // TOPK_HALF, -1) * N_KEYS
             + jnp.take_along_axis(i2, ci % TOPK_HALF, -1))
        gate = jax.nn.softmax(sc, axis=-1)
        act = jax.nn.gelu(jnp.einsum('td,thkd->thk', hb, u_tab[e]).astype(f32), approximate=False)
        return jnp.einsum('thk,thkd->td', (gate * act).astype(hb.dtype), v_tab[e])

    out = lax.map(blk, hb_all).reshape(-1, D)[:n]
    return out.reshape(B, T, D)


def setup_inputs(seed: int = 0) -> dict:
    key = jax.random.key(seed)
    ks = iter(jax.random.split(key, 64))
    f32 = jnp.float32

    def nrm(shape, scale):
        return jax.random.normal(next(ks), shape, f32) * scale

    def gain(shape):
        return 1.0 + nrm(shape, 0.02)

    n_pages = PAST_LEN // PAGE_SIZE
    n_used = DEC_BATCH * n_pages
    n_pool = n_used + max(1, n_used // 4)
    page_table = jax.random.permutation(next(ks), n_pool)[:n_used].reshape(DEC_BATCH, n_pages).astype(jnp.int32)
    L, D = DEPTH, D_MODEL
    return {
        'x_prompt': nrm((BATCH, SEQ, D), 1.0),
        'x_sample': nrm((DEC_BATCH, DEC_SEQ, D), 1.0),
        'mem_prompt': nrm((BATCH, N_MEM, D), 1.0),
        'cache_k': nrm((L, n_pool, PAGE_SIZE, H_DIFF, 2 * DK_DIFF), 1.0),
        'cache_v': nrm((L, n_pool, PAGE_SIZE, H_DIFF, DV_DIFF), 1.0),
        'page_table': page_table,
        'cache_mem_k': nrm((L, DEC_BATCH, N_MEM, H_MEM, DH_MEM), 1.0),
        'cache_mem_v': nrm((L, DEC_BATCH, N_MEM, H_MEM, DH_MEM), 1.0),
        'state_conv': nrm((L, DEC_BATCH, CONV_WIDTH - 1, C_CONV), 0.5),
        'state_shift': nrm((L, DEC_BATCH, D), 1.0),
        'state_wkv': nrm((L, DEC_BATCH, H_RWKV, N_RWKV, N_RWKV), 0.3),
        'norm_mix': gain((L, D)),
        'w_in': nrm((L, D, D_IN_TOTAL), D ** -0.5),
        'conv_w': nrm((L, CONV_WIDTH, C_CONV), CONV_WIDTH ** -0.5),
        'conv_b': nrm((L, C_CONV), 0.01),
        'conv_ln_g': gain((L, C_CONV)),
        'conv_ln_b': nrm((L, C_CONV), 0.01),
        'w_conv_out': nrm((L, C_CONV, D), C_CONV ** -0.5),
        'rwkv_mu': jax.random.uniform(next(ks), (L, W_RWKV_IN), f32),
        'rwkv_w0': 0.5 + nrm((L, C_RWKV), 0.3),
        'rwkv_w2': nrm((L, LORA_DECAY, C_RWKV), 0.1 * LORA_DECAY ** -0.5),
        'rwkv_a0': nrm((L, C_RWKV), 0.1),
        'rwkv_a2': nrm((L, LORA_ICLR, C_RWKV), 0.1 * LORA_ICLR ** -0.5),
        'rwkv_g2': nrm((L, LORA_GATE, C_RWKV), LORA_GATE ** -0.5),
        'rwkv_kk': 0.85 + nrm((L, C_RWKV), 0.02),
        'rwkv_ka': 1.0 + nrm((L, C_RWKV), 0.02),
        'rwkv_rk': nrm((L, H_RWKV, N_RWKV), 0.1),
        'rwkv_gn_g': gain((L, C_RWKV)),
        'rwkv_gn_b': nrm((L, C_RWKV), 0.01),
        'w_rwkv_out': nrm((L, C_RWKV, D), C_RWKV ** -0.5),
        'attn_lq1': nrm((L, DK_DIFF), 0.1),
        'attn_lk1': nrm((L, DK_DIFF), 0.1),
        'attn_lq2': nrm((L, DK_DIFF), 0.1),
        'attn_lk2': nrm((L, DK_DIFF), 0.1),
        'attn_subln': gain((L, DV_DIFF)),
        'w_attn_out': nrm((L, C_DIFF, D), C_DIFF ** -0.5),
        'w_o': nrm((L, D, D), D ** -0.5),
        'norm_cross': gain((L, D)),
        'norm_mem': gain((L, D)),
        'w_cq': nrm((L, D, D), D ** -0.5),
        'w_mk': nrm((L, D, D), D ** -0.5),
        'w_mv': nrm((L, D, D), D ** -0.5),
        'w_co': nrm((L, D, D), D ** -0.5),
        'norm_ffn': gain((L, D)),
        'w_pq': nrm((L, D, H_PEER * D_PEER_Q), D ** -0.5),
        'peer_k1': nrm((L, N_KEYS, D_HALF), D_HALF ** -0.5),
        'peer_k2': nrm((L, N_KEYS, D_HALF), D_HALF ** -0.5),
        'peer_u': nrm((L, N_EXPERTS, D), D ** -0.5),
        'peer_v': nrm((L, N_EXPERTS, D), 0.3),
        'norm_final': gain((D,)),
    }


def reference(x_prompt, x_sample, mem_prompt, cache_k, cache_v, page_table, cache_mem_k, cache_mem_v,
              state_conv, state_shift, state_wkv, norm_mix, w_in, conv_w, conv_b, conv_ln_g, conv_ln_b,
              w_conv_out, rwkv_mu, rwkv_w0, rwkv_w2, rwkv_a0, rwkv_a2, rwkv_g2, rwkv_kk, rwkv_ka, rwkv_rk,
              rwkv_gn_g, rwkv_gn_b, w_rwkv_out, attn_lq1, attn_lk1, attn_lq2, attn_lk2, attn_subln,
              w_attn_out, w_o, norm_cross, norm_mem, w_cq, w_mk, w_mv, w_co, norm_ffn, w_pq, peer_k1,
              peer_k2, peer_u, peer_v, norm_final):
    xp, xs = x_prompt, x_sample
    Bp, Bs = xp.shape[0], xs.shape[0]
    pk, pv, pmk, pmv, pconv, pshift, pwkv = [], [], [], [], [], [], []
    sk, sv, sconv, sshift, swkv = [], [], [], [], []
    for l in range(DEPTH):
        lp = {
            'w_in': w_in[l], 'conv_w': conv_w[l], 'conv_b': conv_b[l], 'conv_ln_g': conv_ln_g[l],
            'conv_ln_b': conv_ln_b[l], 'w_conv_out': w_conv_out[l], 'rwkv_mu': rwkv_mu[l],
            'rwkv_w0': rwkv_w0[l], 'rwkv_w2': rwkv_w2[l], 'rwkv_a0': rwkv_a0[l], 'rwkv_a2': rwkv_a2[l],
            'rwkv_g2': rwkv_g2[l], 'rwkv_kk': rwkv_kk[l], 'rwkv_ka': rwkv_ka[l], 'rwkv_rk': rwkv_rk[l],
            'rwkv_gn_g': rwkv_gn_g[l], 'rwkv_gn_b': rwkv_gn_b[l], 'w_rwkv_out': w_rwkv_out[l],
            'attn_subln': attn_subln[l], 'w_attn_out': w_attn_out[l], 'w_o': w_o[l],
        }
        lam_init = 0.8 - 0.6 * math.exp(-0.3 * l)
        lam = diff_lambda(attn_lq1[l], attn_lk1[l], attn_lq2[l], attn_lk2[l], lam_init)

        h = rmsnorm(xp, norm_mix[l])
        mix, cbuf, wkv, k_new, v_new = hybrid_mix(
            h, jnp.zeros((Bp, 1, W_RWKV_IN), h.dtype), jnp.zeros((Bp, CONV_WIDTH - 1, C_CONV), h.dtype),
            jnp.zeros((Bp, H_RWKV, N_RWKV, N_RWKV), jnp.float32), diff_attn_prompt, lp, lam, lam_init)
        xp = xp + mix
        mk, mv = memory_kv(mem_prompt, norm_mem[l], w_mk[l], w_mv[l])
        xp = xp + cross_attn(xp, mk, mv, norm_cross[l], w_cq[l], w_co[l])
        xp = xp + peer_ffn(xp, norm_ffn[l], w_pq[l], peer_k1[l], peer_k2[l], peer_u[l], peer_v[l])
        pk.append(k_new); pv.append(v_new); pmk.append(mk); pmv.append(mv)
        pconv.append(cbuf); pshift.append(h[:, -1]); pwkv.append(wkv.astype(xp.dtype))

        h = rmsnorm(xs, norm_mix[l])
        z_prev = state_shift[l][:, None, :].astype(h.dtype) @ w_in[l][:, OFF_RWKV:OFF_DIFF]
        attend_s = functools.partial(diff_attn_sample, cache_k=cache_k, cache_v=cache_v,
                                     page_table=page_table, layer=l)
        mix, cbuf, wkv, k_new, v_new = hybrid_mix(h, z_prev, state_conv[l], state_wkv[l], attend_s, lp, lam, lam_init)
        xs = xs + mix
        xs = xs + cross_attn(xs, cache_mem_k[l], cache_mem_v[l], norm_cross[l], w_cq[l], w_co[l])
        xs = xs + peer_ffn(xs, norm_ffn[l], w_pq[l], peer_k1[l], peer_k2[l], peer_u[l], peer_v[l])
        sk.append(k_new); sv.append(v_new); sconv.append(cbuf)
        sshift.append(h[:, -1]); swkv.append(wkv.astype(state_wkv.dtype))

    y_prompt = rmsnorm(xp, norm_final)
    y_sample = rmsnorm(xs, norm_final)
    return (y_prompt, y_sample, jnp.stack(pk), jnp.stack(pv), jnp.stack(pmk), jnp.stack(pmv),
            jnp.stack(pconv), jnp.stack(pshift), jnp.stack(pwkv), jnp.stack(sk), jnp.stack(sv),
            jnp.stack(sconv), jnp.stack(sshift), jnp.stack(swkv))
```

```python
import functools
import math

import jax
import jax.numpy as jnp
from jax import lax
from jax.experimental import pallas as pl
from jax.experimental.pallas import tpu as pltpu

F32 = jnp.float32
BF16 = jnp.bfloat16

LANES = 128
SUBLANES = 8
VMEM_LIMIT_BYTES = 56 * 1024 * 1024

D_MODEL = 1024
CONV_WIDTH = 31
CONV_HALO = CONV_WIDTH - 1
CONV_PAD = 32
N_RWKV = 64
H_RWKV = D_MODEL // N_RWKV
LORA_DECAY = 64
LORA_ICLR = 64
LORA_GATE = 128
LORA_ALL = LORA_DECAY + LORA_ICLR + LORA_GATE
RWKV_GN_EPS = 64e-5
DK_DIFF = 64
DV_DIFF = 128
H_DIFF = D_MODEL // DV_DIFF
DIFF_SCALE = DK_DIFF ** -0.5
N_MEM = 256
H_MEM = 4
DH_MEM = D_MODEL // H_MEM
N_KEYS = 128
H_PEER = 8
D_HALF = 128
TOPK = 16
TOPK_SHIFT = TOPK.bit_length() - 1
PAGE_SIZE = 128
PAGES_PER_STEP = 8
EPS = 1e-6

COL_CONV_A, COL_CONV_B, COL_R, COL_K, COL_V = 0, 1024, 2048, 3072, 4096
COL_DQ, COL_DK, COL_DV = 5120, 6144, 7168
COL_GATE = 8192
COL_LORA = 11264
D_IN_TOTAL = 11520
W_RWKV_IN = 3 * D_MODEL + LORA_ALL


def _params(*sem):
    return pltpu.CompilerParams(dimension_semantics=sem, vmem_limit_bytes=VMEM_LIMIT_BYTES)


def _tile(n, pref):
    if n <= pref:
        return n
    t = pref
    while n % t:
        t //= 2
    return t


def _rms(x, g, eps):
    return x * lax.rsqrt(jnp.mean(x * x, axis=-1, keepdims=True) + eps) * g


def _dot(a, b):
    return jnp.dot(a, b, preferred_element_type=F32)


def _dot_nt(a, b):
    return lax.dot_general(a, b, (((1,), (1,)), ((), ())), preferred_element_type=F32)


def _mm_kernel(*refs, has_norm):
    if has_norm:
        x_ref, g_ref, w_ref, o_ref, xs_ref = refs
    else:
        x_ref, w_ref, o_ref, xs_ref = refs

    @pl.when(pl.program_id(1) == 0)
    def _():
        x = x_ref[...]
        if has_norm:
            x = _rms(x, g_ref[...], EPS)
        xs_ref[...] = x.astype(BF16)

    o_ref[...] = _dot(xs_ref[...], w_ref[...])


def _matmul(x, w_bf16, norm_g=None, tn_pref=1280):
    m, k = x.shape
    n = w_bf16.shape[1]
    tm = _tile(m, 512)
    tn = n
    for cand in (tn_pref, 1664, 1024, 512, 256, 128):
        if n % cand == 0:
            tn = cand
            break
    in_specs = [pl.BlockSpec((tm, k), lambda i, j: (i, 0))]
    args = [x]
    if norm_g is not None:
        in_specs.append(pl.BlockSpec((1, k), lambda i, j: (0, 0)))
        args.append(norm_g.reshape(1, k))
    in_specs.append(pl.BlockSpec((k, tn), lambda i, j: (0, j)))
    args.append(w_bf16)
    return pl.pallas_call(
        functools.partial(_mm_kernel, has_norm=norm_g is not None),
        grid=(m // tm, n // tn),
        in_specs=in_specs,
        out_specs=pl.BlockSpec((tm, tn), lambda i, j: (i, j)),
        out_shape=jax.ShapeDtypeStruct((m, n), F32),
        scratch_shapes=[pltpu.VMEM((tm, k), BF16)],
        compiler_params=_params("parallel", "arbitrary"),
        name="matmul",
    )(*args)


def _rmsnorm_kernel(x_ref, g_ref, o_ref):
    o_ref[...] = _rms(x_ref[...], g_ref[...], EPS)


def _rmsnorm(x, g):
    m, k = x.shape
    tm = _tile(m, 1024)
    return pl.pallas_call(
        _rmsnorm_kernel,
        grid=(m // tm,),
        in_specs=[pl.BlockSpec((tm, k), lambda i: (i, 0)), pl.BlockSpec((1, k), lambda i: (0, 0))],
        out_specs=pl.BlockSpec((tm, k), lambda i: (i, 0)),
        out_shape=jax.ShapeDtypeStruct((m, k), F32),
        compiler_params=_params("parallel"),
        name="rmsnorm",
    )(x, g.reshape(1, k))


def _conv_kernel(a_ref, b_ref, st_ref, w_ref, cb_ref, g_ref, be_ref, act_ref, ns_ref, ext_ref, *, tt, sub):
    @pl.when(pl.program_id(1) == 0)
    def _():
        ext_ref[0:CONV_PAD - CONV_HALO, :] = jnp.zeros((CONV_PAD - CONV_HALO, D_MODEL), F32)
        ext_ref[CONV_PAD - CONV_HALO:CONV_PAD, :] = st_ref[0]

    ext_ref[CONV_PAD:CONV_PAD + tt, :] = a_ref[0] * jax.nn.sigmoid(b_ref[0])
    base = CONV_PAD - CONV_HALO
    for s in range(tt // sub):
        acc = jnp.zeros((sub, D_MODEL), F32)
        for j in range(CONV_WIDTH):
            lo = base + s * sub + j
            acc = acc + ext_ref[lo:lo + sub, :] * w_ref[j:j + 1, :]
        y = acc + cb_ref[...]
        mu = jnp.mean(y, axis=-1, keepdims=True)
        yc = y - mu
        var = jnp.mean(yc * yc, axis=-1, keepdims=True)
        y = yc * lax.rsqrt(var + 1e-5) * g_ref[...] + be_ref[...]
        act_ref[0, s * sub:(s + 1) * sub, :] = y * jax.nn.sigmoid(y)
    tail = ext_ref[tt + base:tt + CONV_PAD, :]
    ns_ref[0] = tail
    ext_ref[base:CONV_PAD, :] = tail


def _conv_branch(z3, conv_state, conv_w, conv_b, ln_g, ln_b):
    b, t, _ = z3.shape
    tt = _tile(t, 128)
    sub = min(tt, 32)
    row = lambda v: v.reshape(1, D_MODEL)
    cspec = pl.BlockSpec((1, D_MODEL), lambda i, j: (0, 0))
    return pl.pallas_call(
        functools.partial(_conv_kernel, tt=tt, sub=sub),
        grid=(b, t // tt),
        in_specs=[
            pl.BlockSpec((1, tt, D_MODEL), lambda i, j: (i, j, COL_CONV_A // D_MODEL)),
            pl.BlockSpec((1, tt, D_MODEL), lambda i, j: (i, j, COL_CONV_B // D_MODEL)),
            pl.BlockSpec((1, CONV_HALO, D_MODEL), lambda i, j: (i, 0, 0)),
            pl.BlockSpec((CONV_WIDTH, D_MODEL), lambda i, j: (0, 0)),
            cspec, cspec, cspec,
        ],
        out_specs=[
            pl.BlockSpec((1, tt, D_MODEL), lambda i, j: (i, j, 0)),
            pl.BlockSpec((1, CONV_HALO, D_MODEL), lambda i, j: (i, 0, 0)),
        ],
        out_shape=[jax.ShapeDtypeStruct((b, t, D_MODEL), F32), jax.ShapeDtypeStruct((b, CONV_HALO, D_MODEL), F32)],
        scratch_shapes=[pltpu.VMEM((CONV_PAD + tt, D_MODEL), F32)],
        compiler_params=_params("parallel", "arbitrary"),
        name="conv_branch",
    )(z3, z3, conv_state, conv_w, row(conv_b), row(ln_g), row(ln_b))


def _rwkv_prep_kernel(zr_ref, zk_ref, zv_ref, zl_ref, pr_ref, pk_ref, pv_ref, pl_ref,
                      mur_ref, muk_ref, muv_ref, mul_ref, w0_ref, a0_ref, kkp_ref, ka_ref,
                      w2_ref, a2_ref, g2_ref, ones_ref,
                      r_out, w_out, k_out, v_out, kk_out, b_out, g_out,
                      cr_ref, ck_ref, cv_ref, cl_ref):
    @pl.when(pl.program_id(1) == 0)
    def _():
        cr_ref[...] = pr_ref[0]
        ck_ref[...] = pk_ref[0]
        cv_ref[...] = pv_ref[0]
        cl_ref[...] = pl_ref[0]

    def shift_mix(z_ref, carry_ref, mu_ref):
        z = z_ref[0]
        rows = lax.broadcasted_iota(jnp.int32, z.shape, 0)
        prev = jnp.where(rows == 0, carry_ref[...], pltpu.roll(z, 1, axis=0))
        carry_ref[...] = z[z.shape[0] - 1:, :]
        return z + (prev - z) * mu_ref[...]

    r = shift_mix(zr_ref, cr_ref, mur_ref)
    k = shift_mix(zk_ref, ck_ref, muk_ref)
    v = shift_mix(zv_ref, cv_ref, muv_ref)
    lo = shift_mix(zl_ref, cl_ref, mul_ref)

    wl = w0_ref[...] + _dot(jnp.tanh(lo).astype(BF16), w2_ref[...])
    nw = -wl
    w = -(jnp.maximum(nw, 0.0) + jnp.log(1.0 + jnp.exp(-jnp.abs(nw)))) - 0.5
    decay = jnp.exp(-jnp.exp(w))
    a = jax.nn.sigmoid(a0_ref[...] + _dot(lo.astype(BF16), a2_ref[...]))
    g = _dot(jax.nn.sigmoid(lo).astype(BF16), g2_ref[...])
    kkv = k * kkp_ref[...]
    sq = kkv * kkv
    sq_hi = sq.astype(BF16)
    sq_lo = (sq - sq_hi.astype(F32)).astype(BF16)
    ss = _dot(sq_hi, ones_ref[...]) + _dot(sq_lo, ones_ref[...])
    kk = kkv * lax.rsqrt(jnp.maximum(ss, 1e-24))
    r_out[0] = r
    w_out[0] = decay
    k_out[0] = k * (1.0 + (a - 1.0) * ka_ref[...])
    v_out[0] = v
    kk_out[0] = kk
    b_out[0] = kk * a
    g_out[0] = g


def _rwkv_prep(z3, z_prev, mu, w0, w2p, a0, a2p, g2p, kkp, ka, ones_bd):
    b, t, _ = z3.shape
    tt = _tile(t, 256)
    wide = lambda c: pl.BlockSpec((1, tt, D_MODEL), lambda i, j: (i, j, c // D_MODEL))
    prev = lambda c: pl.BlockSpec((1, 1, D_MODEL), lambda i, j: (i, 0, c // D_MODEL))
    vec = pl.BlockSpec((1, D_MODEL), lambda i, j: (0, 0))
    vecl = pl.BlockSpec((1, LORA_ALL), lambda i, j: (0, 0))
    lora_w = pl.BlockSpec((LORA_ALL, D_MODEL), lambda i, j: (0, 0))
    row = lambda v_: v_.reshape(1, -1)
    out = jax.ShapeDtypeStruct((b, t, D_MODEL), F32)
    return pl.pallas_call(
        _rwkv_prep_kernel,
        grid=(b, t // tt),
        in_specs=[
            wide(COL_R), wide(COL_K), wide(COL_V),
            pl.BlockSpec((1, tt, LORA_ALL), lambda i, j: (i, j, COL_LORA // LORA_ALL)),
            prev(0), prev(D_MODEL), prev(2 * D_MODEL),
            pl.BlockSpec((1, 1, LORA_ALL), lambda i, j: (i, 0, 3 * D_MODEL // LORA_ALL)),
            vec, vec, vec, vecl, vec, vec, vec, vec,
            lora_w, lora_w, lora_w,
            pl.BlockSpec((D_MODEL, D_MODEL), lambda i, j: (0, 0)),
        ],
        out_specs=[pl.BlockSpec((1, tt, D_MODEL), lambda i, j: (i, j, 0))] * 7,
        out_shape=[out] * 7,
        scratch_shapes=[pltpu.VMEM((1, D_MODEL), F32)] * 3 + [pltpu.VMEM((1, LORA_ALL), F32)],
        compiler_params=_params("parallel", "arbitrary"),
        name="rwkv_prep",
    )(z3, z3, z3, z3, z_prev, z_prev, z_prev, z_prev,
      row(mu[:D_MODEL]), row(mu[D_MODEL:2 * D_MODEL]), row(mu[2 * D_MODEL:3 * D_MODEL]), row(mu[3 * D_MODEL:]),
      row(w0), row(a0), row(kkp), row(ka), w2p, a2p, g2p, ones_bd)


def _rwkv_scan_kernel(r_ref, w_ref, k_ref, v_ref, kk_ref, b_ref, s0_ref, rk_ref, gg_ref, gb_ref,
                      y_ref, sf_ref, s_ref, *, tc):
    @pl.when(pl.program_id(1) == 0)
    def _():
        s_ref[...] = s0_ref[...]

    def step(t, carry):
        sa = jnp.zeros((N_RWKV, LANES), F32)
        for kx in range(N_RWKV):
            sa = sa + s_ref[kx] * kk_ref[t, kx:kx + 1, :]
        vt = v_ref[t]
        y = jnp.zeros((N_RWKV, LANES), F32)
        for kx in range(N_RWKV):
            s_new = (s_ref[kx] * w_ref[t, kx:kx + 1, :] - sa * b_ref[t, kx:kx + 1, :]
                     + vt * k_ref[t, kx:kx + 1, :])
            s_ref[kx] = s_new
            y = y + s_new * r_ref[t, kx:kx + 1, :]
        bonus = jnp.sum(r_ref[t] * k_ref[t] * rk_ref[...], axis=0, keepdims=True)
        mu = jnp.mean(y, axis=0, keepdims=True)
        yc = y - mu
        var = jnp.mean(yc * yc, axis=0, keepdims=True)
        y_ref[t] = yc * lax.rsqrt(var + RWKV_GN_EPS) * gg_ref[...] + gb_ref[...] + bonus * vt
        return carry

    lax.fori_loop(0, tc, step, 0)

    @pl.when(pl.program_id(1) == pl.num_programs(1) - 1)
    def _():
        sf_ref[...] = s_ref[...]


def _rwkv_scan(r, w, k, v, kk, b, s0, rk, gg, gb):
    t, n, l = r.shape
    tc = _tile(t, 32)
    seq = pl.BlockSpec((tc, n, LANES), lambda i, j: (j, 0, i))
    cst = pl.BlockSpec((n, LANES), lambda i, j: (0, i))
    st = pl.BlockSpec((n, n, LANES), lambda i, j: (0, 0, i))
    return pl.pallas_call(
        functools.partial(_rwkv_scan_kernel, tc=tc),
        grid=(l // LANES, t // tc),
        in_specs=[seq] * 6 + [st, cst, cst, cst],
        out_specs=[seq, st],
        out_shape=[jax.ShapeDtypeStruct((t, n, l), F32), jax.ShapeDtypeStruct((n, n, l), F32)],
        scratch_shapes=[pltpu.VMEM((n, n, LANES), F32)],
        compiler_params=_params("parallel", "arbitrary"),
        name="rwkv_scan",
    )(r, w, k, v, kk, b, s0, rk, gg, gb)


def _diff_lambda(lq1_ref, lk1_ref, lq2_ref, lk2_ref, lam_init):
    return (jnp.exp(jnp.sum(lq1_ref[...] * lk1_ref[...], axis=-1, keepdims=True))
            - jnp.exp(jnp.sum(lq2_ref[...] * lk2_ref[...], axis=-1, keepdims=True)) + lam_init)


def _subln(o, g_ref, lam_init):
    return o * lax.rsqrt(jnp.mean(o * o, axis=-1, keepdims=True) + 1e-5) * g_ref[...] * (1.0 - lam_init)


def _flash_kernel(q_ref, k_ref, v_ref, g_ref, lq1_ref, lk1_ref, lq2_ref, lk2_ref, o_ref,
                  m_ref, l_ref, acc_ref, *, tq, tk, lam_init):
    qi, ki = pl.program_id(2), pl.program_id(3)

    @pl.when(ki == 0)
    def _():
        m_ref[...] = jnp.full(m_ref.shape, -jnp.inf, F32)
        l_ref[...] = jnp.zeros(l_ref.shape, F32)
        acc_ref[...] = jnp.zeros(acc_ref.shape, F32)

    @pl.when(ki * tk <= qi * tq + tq - 1)
    def _():
        q = q_ref[0]
        kb = k_ref[0].astype(BF16)
        vb = v_ref[0].astype(BF16)
        lane = lax.broadcasted_iota(jnp.int32, q.shape, 1)
        qpos = qi * tq + lax.broadcasted_iota(jnp.int32, (tq, tk), 0)
        kpos = ki * tk + lax.broadcasted_iota(jnp.int32, (tq, tk), 1)
        mask = kpos <= qpos
        for mi in range(2):
            qm = jnp.where((lane >= mi * DK_DIFF) & (lane < (mi + 1) * DK_DIFF), q, 0.0).astype(BF16)
            s = jnp.where(mask, _dot_nt(qm, kb) * DIFF_SCALE, -jnp.inf)
            m_prev = m_ref[mi]
            m_new = jnp.maximum(m_prev, jnp.max(s, axis=-1, keepdims=True))
            p = jnp.exp(s - m_new)
            alpha = jnp.exp(m_prev - m_new)
            l_ref[mi] = alpha * l_ref[mi] + jnp.sum(p, axis=-1, keepdims=True)
            acc_ref[mi] = alpha * acc_ref[mi] + _dot(p.astype(BF16), vb)
            m_ref[mi] = m_new

    @pl.when(ki == pl.num_programs(3) - 1)
    def _():
        lam = _diff_lambda(lq1_ref, lk1_ref, lq2_ref, lk2_ref, lam_init)
        o = acc_ref[0] / l_ref[0] - lam * (acc_ref[1] / l_ref[1])
        o_ref[0] = _subln(o, g_ref, lam_init)


def _diff_attn_prompt(z3, subln_g, lq1, lk1, lq2, lk2, lam_init):
    b, t, _ = z3.shape
    tq = tk = _tile(t, 512)
    qblk, kblk, vblk = COL_DQ // DV_DIFF, COL_DK // DV_DIFF, COL_DV // DV_DIFF
    last_k = lambda qi, ki: jnp.minimum(ki, (qi * tq + tq - 1) // tk)
    small = pl.BlockSpec((1, DK_DIFF), lambda bi, h, qi, ki: (0, 0))
    row = lambda v_: v_.reshape(1, -1)
    return pl.pallas_call(
        functools.partial(_flash_kernel, tq=tq, tk=tk, lam_init=lam_init),
        grid=(b, H_DIFF, t // tq, t // tk),
        in_specs=[
            pl.BlockSpec((1, tq, DV_DIFF), lambda bi, h, qi, ki: (bi, qi, qblk + h)),
            pl.BlockSpec((1, tk, DV_DIFF), lambda bi, h, qi, ki: (bi, last_k(qi, ki), kblk + h)),
            pl.BlockSpec((1, tk, DV_DIFF), lambda bi, h, qi, ki: (bi, last_k(qi, ki), vblk + h)),
            pl.BlockSpec((1, DV_DIFF), lambda bi, h, qi, ki: (0, 0)),
            small, small, small, small,
        ],
        out_specs=pl.BlockSpec((1, tq, DV_DIFF), lambda bi, h, qi, ki: (bi, qi, h)),
        out_shape=jax.ShapeDtypeStruct((b, t, D_MODEL), F32),
        scratch_shapes=[pltpu.VMEM((2, tq, 1), F32), pltpu.VMEM((2, tq, 1), F32), pltpu.VMEM((2, tq, DV_DIFF), F32)],
        compiler_params=_params("parallel", "parallel", "parallel", "arbitrary"),
        name="diff_attn_prompt",
    )(z3, z3, z3, row(subln_g), row(lq1), row(lk1), row(lq2), row(lk2))


def _sample_attn_kernel(pt_ref, wt_ref, *refs, n_new, lam_init):
    kp = refs[:PAGES_PER_STEP]
    vp = refs[PAGES_PER_STEP:2 * PAGES_PER_STEP]
    kn_ref, vn_ref, g_ref, lq1_ref, lk1_ref, lq2_ref, lk2_ref, o_ref, m_ref, l_ref, acc_ref = refs[2 * PAGES_PER_STEP:]
    c = pl.program_id(1)
    rows = H_DIFF * 2 * n_new

    @pl.when(c == 0)
    def _():
        m_ref[...] = jnp.full(m_ref.shape, -jnp.inf, F32)
        l_ref[...] = jnp.zeros(l_ref.shape, F32)
        acc_ref[...] = jnp.zeros(acc_ref.shape, F32)

    wt = wt_ref[0].astype(BF16)

    def online(s, values):
        m_prev = m_ref[...]
        m_new = jnp.maximum(m_prev, jnp.max(s, axis=-1, keepdims=True))
        p = jnp.exp(s - m_new)
        alpha = jnp.exp(m_prev - m_new)
        l_ref[...] = alpha * l_ref[...] + jnp.sum(p, axis=-1, keepdims=True)
        pv = None
        for off, vb in values:
            term = _dot(p[:, off:off + PAGE_SIZE].astype(BF16), vb)
            pv = term if pv is None else pv + term
        acc_ref[...] = alpha * acc_ref[...] + pv
        m_ref[...] = m_new

    s = jnp.concatenate([_dot_nt(wt, kr[0, 0].astype(BF16)) for kr in kp], axis=1) * DIFF_SCALE
    online(s, [(j * PAGE_SIZE, vr[0, 0].astype(BF16)) for j, vr in enumerate(vp)])

    @pl.when(c == pl.num_programs(1) - 1)
    def _():
        sn = _dot_nt(wt, kn_ref[0].astype(BF16)) * DIFF_SCALE
        qidx = lax.broadcasted_iota(jnp.int32, sn.shape, 0) % n_new
        col = lax.broadcasted_iota(jnp.int32, sn.shape, 1)
        online(jnp.where(col <= qidx, sn, -jnp.inf), [(0, vn_ref[0].astype(BF16))])
        lam = _diff_lambda(lq1_ref, lk1_ref, lq2_ref, lk2_ref, lam_init)
        o_all = acc_ref[...] / l_ref[...]
        for h in range(H_DIFF):
            r0 = h * 2 * n_new
            o1 = o_all[r0:r0 + n_new, h * DV_DIFF:(h + 1) * DV_DIFF]
            o2 = o_all[r0 + n_new:r0 + 2 * n_new, h * DV_DIFF:(h + 1) * DV_DIFF]
            o_ref[0, :, h * DV_DIFF:(h + 1) * DV_DIFF] = _subln(o1 - lam * o2, g_ref, lam_init)


def _diff_attn_sample(wt, k_new_pad, v_new_pad, cache_k4, cache_v4, page_table, layer, subln_g,
                      lq1, lk1, lq2, lk2, lam_init, n_new):
    b, rows, _ = wt.shape
    n_pages = page_table.shape[1]
    n_steps = n_pages // PAGES_PER_STEP

    def page_spec(j):
        return pl.BlockSpec((1, 1, PAGE_SIZE, D_MODEL),
                            lambda bi, c, pt: (layer, pt[bi * n_pages + c * PAGES_PER_STEP + j], 0, 0))

    small = pl.BlockSpec((1, DK_DIFF), lambda bi, c, pt: (0, 0))
    new_spec = pl.BlockSpec((1, PAGE_SIZE, D_MODEL), lambda bi, c, pt: (bi, 0, 0))
    row = lambda v_: v_.reshape(1, -1)
    return pl.pallas_call(
        functools.partial(_sample_attn_kernel, n_new=n_new, lam_init=lam_init),
        grid_spec=pltpu.PrefetchScalarGridSpec(
            num_scalar_prefetch=1,
            grid=(b, n_steps),
            in_specs=[pl.BlockSpec((1, rows, D_MODEL), lambda bi, c, pt: (bi, 0, 0))]
            + [page_spec(j) for j in range(PAGES_PER_STEP)] * 2
            + [new_spec, new_spec, pl.BlockSpec((1, DV_DIFF), lambda bi, c, pt: (0, 0)), small, small, small, small],
            out_specs=pl.BlockSpec((1, n_new, D_MODEL), lambda bi, c, pt: (bi, 0, 0)),
            scratch_shapes=[pltpu.VMEM((rows, 1), F32), pltpu.VMEM((rows, 1), F32), pltpu.VMEM((rows, D_MODEL), F32)],
        ),
        out_shape=jax.ShapeDtypeStruct((b, n_new, D_MODEL), F32),
        compiler_params=_params("parallel", "arbitrary"),
        name="diff_attn_sample",
    )(page_table.reshape(-1), wt, *([cache_k4] * PAGES_PER_STEP), *([cache_v4] * PAGES_PER_STEP),
      k_new_pad, v_new_pad, row(subln_g), row(lq1), row(lk1), row(lq2), row(lk2))


def _mix_out_kernel(x_ref, ca_ref, ys_ref, gr_ref, da_ref, g0_ref, g1_ref, g2_ref,
                    wc_ref, wr_ref, wd_ref, wo_ref, o_ref):
    yc = _dot(ca_ref[...].astype(BF16), wc_ref[...])
    yr = _dot((ys_ref[...] * gr_ref[...]).astype(BF16), wr_ref[...])
    yd = _dot(da_ref[...].astype(BF16), wd_ref[...])
    merged = (jax.nn.sigmoid(g0_ref[...]) * yc + jax.nn.sigmoid(g1_ref[...]) * yr
              + jax.nn.sigmoid(g2_ref[...]) * yd)
    o_ref[...] = x_ref[...] + _dot(merged.astype(BF16), wo_ref[...])


def _mix_out(x2, conv_act, y_scan, g_rwkv, diff_act, z2, wc, wr, wd, wo):
    m = x2.shape[0]
    tm = _tile(m, 256)
    rowblk = pl.BlockSpec((tm, D_MODEL), lambda i: (i, 0))
    gate = lambda n: pl.BlockSpec((tm, D_MODEL), lambda i: (i, COL_GATE // D_MODEL + n))
    wspec = pl.BlockSpec((D_MODEL, D_MODEL), lambda i: (0, 0))
    return pl.pallas_call(
        _mix_out_kernel,
        grid=(m // tm,),
        in_specs=[rowblk] * 5 + [gate(0), gate(1), gate(2)] + [wspec] * 4,
        out_specs=rowblk,
        out_shape=jax.ShapeDtypeStruct((m, D_MODEL), F32),
        compiler_params=_params("parallel"),
        name="mix_out",
    )(x2, conv_act, y_scan, g_rwkv, diff_act, z2, z2, z2, wc, wr, wd, wo)


def _cross_kernel(x_ref, mk_ref, mv_ref, g_ref, wq_ref, wo_ref, o_ref):
    x = x_ref[0]
    q = _dot(_rms(x, g_ref[...], EPS).astype(BF16), wq_ref[...])
    mk = mk_ref[0].astype(BF16)
    mv = mv_ref[0].astype(BF16)
    heads = []
    for h in range(H_MEM):
        sl = slice(h * DH_MEM, (h + 1) * DH_MEM)
        s = _dot_nt(q[:, sl].astype(BF16), mk[:, sl]) * (DH_MEM ** -0.5)
        p = jnp.exp(s - jnp.max(s, axis=-1, keepdims=True))
        p = p / jnp.sum(p, axis=-1, keepdims=True)
        heads.append(_dot(p.astype(BF16), mv[:, sl]))
    o = jnp.concatenate(heads, axis=1)
    o_ref[0] = x + _dot(o.astype(BF16), wo_ref[...])


def _cross_attn(x3, mk, mv, norm_g, wq, wo):
    b, t, _ = x3.shape
    tm = _tile(t, 512)
    wspec = pl.BlockSpec((D_MODEL, D_MODEL), lambda i, j: (0, 0))
    mem = pl.BlockSpec((1, N_MEM, D_MODEL), lambda i, j: (i, 0, 0))
    xs = pl.BlockSpec((1, tm, D_MODEL), lambda i, j: (i, j, 0))
    return pl.pallas_call(
        _cross_kernel,
        grid=(b, t // tm),
        in_specs=[xs, mem, mem, pl.BlockSpec((1, D_MODEL), lambda i, j: (0, 0)), wspec, wspec],
        out_specs=xs,
        out_shape=jax.ShapeDtypeStruct((b, t, D_MODEL), F32),
        compiler_params=_params("parallel", "parallel"),
        name="cross_attn",
    )(x3, mk, mv, norm_g.reshape(1, D_MODEL), wq, wo)


def _top16(x, n_rows):
    rows = lax.broadcasted_iota(jnp.int32, x.shape, 0)
    rank = lax.broadcasted_iota(jnp.int32, (TOPK, x.shape[1]), 0)
    vals = jnp.zeros((TOPK, x.shape[1]), F32)
    idxs = jnp.zeros((TOPK, x.shape[1]), jnp.int32)
    for it in range(TOPK):
        mx = jnp.max(x, axis=0, keepdims=True)
        sel = jnp.min(jnp.where(x == mx, rows, n_rows), axis=0, keepdims=True)
        vals = jnp.where(rank == it, mx, vals)
        idxs = jnp.where(rank == it, sel, idxs)
        x = jnp.where(rows == sel, -jnp.inf, x)
    return vals, idxs


def _pick(table, idx):
    out = jnp.zeros(idx.shape, jnp.int32)
    for a in range(TOPK):
        out = out + jnp.where(idx == a, table[a:a + 1, :], 0)
    return out


def _peer_route_kernel(x_ref, g_ref, wq_ref, k1_ref, k2_ref, e_ref, gate_ref):
    h = _rms(x_ref[...], g_ref[...], EPS)
    q = _dot(h.astype(BF16), wq_ref[...])
    k1 = k1_ref[...].astype(BF16)
    k2 = k2_ref[...].astype(BF16)
    for hd in range(H_PEER):
        q1 = q[:, (2 * hd) * D_HALF:(2 * hd + 1) * D_HALF].astype(BF16)
        q2 = q[:, (2 * hd + 1) * D_HALF:(2 * hd + 2) * D_HALF].astype(BF16)
        v1, i1 = _top16(_dot_nt(k1, q1), N_KEYS)
        v2, i2 = _top16(_dot_nt(k2, q2), N_KEYS)
        cand = jnp.concatenate([v1[a:a + 1, :] + v2 for a in range(TOPK)], axis=0)
        sc, ci = _top16(cand, TOPK * TOPK)
        e_ref[hd] = _pick(i1, ci >> TOPK_SHIFT) * N_KEYS + _pick(i2, ci & (TOPK - 1))
        p = jnp.exp(sc - sc[0:1, :])
        gate_ref[hd] = p / jnp.sum(p, axis=0, keepdims=True)


def _peer_route(x2, norm_g, wq, k1, k2):
    m = x2.shape[0]
    tm = _tile(m, 256)
    out_spec = pl.BlockSpec((H_PEER, TOPK, tm), lambda i: (0, 0, i))
    return pl.pallas_call(
        _peer_route_kernel,
        grid=(m // tm,),
        in_specs=[
            pl.BlockSpec((tm, D_MODEL), lambda i: (i, 0)),
            pl.BlockSpec((1, D_MODEL), lambda i: (0, 0)),
            pl.BlockSpec((D_MODEL, H_PEER * 2 * D_HALF), lambda i: (0, 0)),
            pl.BlockSpec((N_KEYS, D_HALF), lambda i: (0, 0)),
            pl.BlockSpec((N_KEYS, D_HALF), lambda i: (0, 0)),
        ],
        out_specs=[out_spec, out_spec],
        out_shape=[jax.ShapeDtypeStruct((H_PEER, TOPK, m), jnp.int32), jax.ShapeDtypeStruct((H_PEER, TOPK, m), F32)],
        compiler_params=_params("parallel"),
        name="peer_route",
    )(x2, norm_g.reshape(1, D_MODEL), wq, k1, k2)


PEER_TOKENS = 8
PEER_PICKS = H_PEER * TOPK
PEER_PICKS_SHIFT = PEER_PICKS.bit_length() - 1
PEER_ROWS = PEER_TOKENS * PEER_PICKS


def _peer_gather_copy(tab_ref, buf_ref, sem_ref, slot, expert, row):
    return pltpu.make_async_copy(tab_ref.at[pl.ds(expert, 1)], buf_ref.at[slot, pl.ds(row, 1)], sem_ref.at[slot])


def _peer_expert_kernel(e_cur_ref, e_nxt_ref, x_ref, g_ref, gate_ref, tab_ref, o_ref, buf_ref, sem_ref):
    i = pl.program_id(0)
    n = pl.num_programs(0)
    slot = i % 2

    def issue(e_ref, to_slot):
        def body(p, carry):
            _peer_gather_copy(tab_ref, buf_ref, sem_ref, to_slot, e_ref[p >> PEER_PICKS_SHIFT, p & (PEER_PICKS - 1)], p).start()
            return carry
        lax.fori_loop(0, PEER_ROWS, body, 0, unroll=8)

    @pl.when(i == 0)
    def _():
        issue(e_cur_ref, 0)

    @pl.when(i + 1 < n)
    def _():
        issue(e_nxt_ref, 1 - slot)

    pltpu.make_async_copy(tab_ref.at[pl.ds(0, PEER_ROWS)], buf_ref.at[slot], sem_ref.at[slot]).wait()

    x = x_ref[...]
    h = _rms(x, g_ref[...], EPS)
    gate_t = jnp.transpose(gate_ref[...])
    for t in range(PEER_TOKENS):
        rows = buf_ref[slot, t * PEER_PICKS:(t + 1) * PEER_PICKS, :]
        hu = jnp.sum(rows[:, :D_MODEL] * h[t:t + 1, :], axis=-1, keepdims=True)
        act = 0.5 * hu * (1.0 + lax.erf(hu * (2.0 ** -0.5)))
        wgt = act * gate_t[:, t:t + 1]
        o_ref[t:t + 1, :] = x[t:t + 1, :] + jnp.sum(wgt * rows[:, D_MODEL:], axis=0, keepdims=True)


def _peer_experts(x2, norm_g, e_tok, gate_tok, table):
    m = x2.shape[0]
    n = m // PEER_TOKENS
    return pl.pallas_call(
        _peer_expert_kernel,
        grid=(n,),
        in_specs=[
            pl.BlockSpec((PEER_TOKENS, PEER_PICKS), lambda i: (i, 0), memory_space=pltpu.SMEM),
            pl.BlockSpec((PEER_TOKENS, PEER_PICKS), lambda i: (jnp.minimum(i + 1, n - 1), 0), memory_space=pltpu.SMEM),
            pl.BlockSpec((PEER_TOKENS, D_MODEL), lambda i: (i, 0)),
            pl.BlockSpec((1, D_MODEL), lambda i: (0, 0)),
            pl.BlockSpec((PEER_TOKENS, PEER_PICKS), lambda i: (i, 0)),
            pl.BlockSpec(memory_space=pl.ANY),
        ],
        out_specs=pl.BlockSpec((PEER_TOKENS, D_MODEL), lambda i: (i, 0)),
        out_shape=jax.ShapeDtypeStruct((m, D_MODEL), F32),
        scratch_shapes=[pltpu.VMEM((2, PEER_ROWS, 2 * D_MODEL), F32), pltpu.SemaphoreType.DMA((2,))],
        compiler_params=_params("arbitrary"),
        name="peer_experts",
    )(e_tok, e_tok, x2, norm_g.reshape(1, D_MODEL), gate_tok, table)


def _to_lanes(a, b, t):
    return a.reshape(b, t, H_RWKV, N_RWKV).transpose(1, 3, 0, 2).reshape(t, N_RWKV, b * H_RWKV)


def _head_const(v, b):
    return jnp.tile(v.reshape(H_RWKV, N_RWKV).T, (1, b))


def _layer(x3, lw, z_prev, conv_state, wkv0_lanes, attend, mk, mv):
    b, t, _ = x3.shape
    m = b * t
    x2 = x3.reshape(m, D_MODEL)
    z2 = _matmul(x2, lw["w_in"], norm_g=lw["norm_mix"])
    z3 = z2.reshape(b, t, D_IN_TOTAL)
    h_last = _rmsnorm(x3[:, -1, :], lw["norm_mix"])

    conv_act, conv_new = _conv_branch(z3, conv_state, lw["conv_w"], lw["conv_b"], lw["conv_ln_g"], lw["conv_ln_b"])

    r, w, k, v, kk, bb, g_rwkv = _rwkv_prep(z3, z_prev, lw["rwkv_mu"], lw["rwkv_w0"], lw["w2p"], lw["rwkv_a0"],
                                            lw["a2p"], lw["g2p"], lw["rwkv_kk"], lw["rwkv_ka"], lw["ones_bd"])
    lanes = lambda a: _to_lanes(a, b, t)
    y_l, s_fin = _rwkv_scan(lanes(r), lanes(w), lanes(k), lanes(v), lanes(kk), lanes(bb), wkv0_lanes,
                            _head_const(lw["rwkv_rk"].reshape(-1), b), _head_const(lw["rwkv_gn_g"], b),
                            _head_const(lw["rwkv_gn_b"], b))
    y_scan = y_l.reshape(t, N_RWKV, b, H_RWKV).transpose(2, 0, 3, 1).reshape(m, D_MODEL)
    wkv_new = s_fin.reshape(N_RWKV, N_RWKV, b, H_RWKV).transpose(2, 3, 1, 0)

    diff_act = attend(z3)
    x2 = _mix_out(x2, conv_act.reshape(m, D_MODEL), y_scan, g_rwkv.reshape(m, D_MODEL), diff_act.reshape(m, D_MODEL),
                  z2, lw["w_conv_out"], lw["w_rwkv_out"], lw["w_attn_out"], lw["w_o"])

    x3 = _cross_attn(x2.reshape(b, t, D_MODEL), mk, mv, lw["norm_cross"], lw["w_cq"], lw["w_co"])
    x2 = x3.reshape(m, D_MODEL)

    e, gate = _peer_route(x2, lw["norm_ffn"], lw["w_pq"], lw["peer_k1"], lw["peer_k2"])
    tok = lambda a: a.reshape(PEER_PICKS, m).T
    x2 = _peer_experts(x2, lw["norm_ffn"], tok(e), tok(gate), lw["peer_uv"])

    k_new = z3[:, :, COL_DK:COL_DK + D_MODEL].reshape(b, t, H_DIFF, DV_DIFF)
    v_new = z3[:, :, COL_DV:COL_DV + D_MODEL].reshape(b, t, H_DIFF, DV_DIFF)
    return x2.reshape(b, t, D_MODEL), conv_new, h_last, wkv_new, k_new, v_new, z3


def _sample_queries(z3):
    b, t, _ = z3.shape
    q = z3[:, :, COL_DQ:COL_DQ + D_MODEL].reshape(b, t, 2 * H_DIFF, DK_DIFF).transpose(0, 2, 1, 3)
    eye = jnp.eye(2 * H_DIFF, dtype=F32)
    return jnp.einsum("bxid,xy->bxiyd", q, eye).reshape(b, 2 * H_DIFF * t, D_MODEL)


def kernel(x_prompt, x_sample, mem_prompt, cache_k, cache_v, page_table, cache_mem_k, cache_mem_v, state_conv, state_shift, state_wkv, norm_mix, w_in, conv_w, conv_b, conv_ln_g, conv_ln_b, w_conv_out, rwkv_mu, rwkv_w0, rwkv_w2, rwkv_a0, rwkv_a2, rwkv_g2, rwkv_kk, rwkv_ka, rwkv_rk, rwkv_gn_g, rwkv_gn_b, w_rwkv_out, attn_lq1, attn_lk1, attn_lq2, attn_lk2, attn_subln, w_attn_out, w_o, norm_cross, norm_mem, w_cq, w_mk, w_mv, w_co, norm_ffn, w_pq, peer_k1, peer_k2, peer_u, peer_v, norm_final):
    depth = w_in.shape[0]
    bp, tp, _ = x_prompt.shape
    bs, ts, _ = x_sample.shape
    n_pool = cache_k.shape[1]
    cache_k4 = cache_k.reshape(depth, n_pool, PAGE_SIZE, D_MODEL)
    cache_v4 = cache_v.reshape(depth, n_pool, PAGE_SIZE, D_MODEL)
    head_ids = jnp.arange(D_MODEL) // N_RWKV
    ones_bd = (head_ids[:, None] == head_ids[None, :]).astype(BF16)
    off_rwkv, off_diff, off_gate = 2 * D_MODEL, 2 * D_MODEL + W_RWKV_IN, 2 * D_MODEL + W_RWKV_IN + 3 * D_MODEL
    lora0 = off_rwkv + 3 * D_MODEL

    xp, xs = x_prompt, x_sample
    outs = {name: [] for name in ("pk", "pv", "pmk", "pmv", "pconv", "pshift", "pwkv", "sk", "sv", "sconv", "sshift", "swkv")}
    for l in range(depth):
        wl = w_in[l]
        zpad = lambda a, lo: jnp.zeros((LORA_ALL, D_MODEL), F32).at[lo:lo + a.shape[0]].set(a).astype(BF16)
        lw = {
            "w_in": jnp.concatenate([wl[:, :lora0], wl[:, off_diff:], wl[:, lora0:off_diff]], axis=1).astype(BF16),
            "norm_mix": norm_mix[l], "conv_w": conv_w[l], "conv_b": conv_b[l], "conv_ln_g": conv_ln_g[l],
            "conv_ln_b": conv_ln_b[l], "rwkv_mu": rwkv_mu[l], "rwkv_w0": rwkv_w0[l], "rwkv_a0": rwkv_a0[l],
            "w2p": zpad(rwkv_w2[l], 0), "a2p": zpad(rwkv_a2[l], LORA_DECAY), "g2p": zpad(rwkv_g2[l], LORA_DECAY + LORA_ICLR),
            "rwkv_kk": rwkv_kk[l], "rwkv_ka": rwkv_ka[l], "rwkv_rk": rwkv_rk[l], "rwkv_gn_g": rwkv_gn_g[l],
            "rwkv_gn_b": rwkv_gn_b[l], "ones_bd": ones_bd,
            "w_conv_out": w_conv_out[l].astype(BF16), "w_rwkv_out": w_rwkv_out[l].astype(BF16),
            "w_attn_out": w_attn_out[l].astype(BF16), "w_o": w_o[l].astype(BF16),
            "norm_cross": norm_cross[l], "w_cq": w_cq[l].astype(BF16), "w_co": w_co[l].astype(BF16),
            "norm_ffn": norm_ffn[l], "w_pq": w_pq[l].astype(BF16), "peer_k1": peer_k1[l], "peer_k2": peer_k2[l],
            "peer_uv": jnp.concatenate([peer_u[l], peer_v[l]], axis=1),
        }
        lam_init = 0.8 - 0.6 * math.exp(-0.3 * l)
        lam_args = (attn_subln[l], attn_lq1[l], attn_lk1[l], attn_lq2[l], attn_lk2[l], lam_init)

        hm = _matmul(mem_prompt.reshape(bp * N_MEM, D_MODEL),
                     jnp.concatenate([w_mk[l], w_mv[l]], axis=1).astype(BF16), norm_g=norm_mem[l], tn_pref=1024)
        mk = hm[:, :D_MODEL].reshape(bp, N_MEM, D_MODEL)
        mv = hm[:, D_MODEL:].reshape(bp, N_MEM, D_MODEL)
        xp, cbuf, hlast, wkv, k_new, v_new, _ = _layer(
            xp, lw, jnp.zeros((bp, 1, W_RWKV_IN), F32), jnp.zeros((bp, CONV_HALO, D_MODEL), F32),
            jnp.zeros((N_RWKV, N_RWKV, bp * H_RWKV), F32),
            lambda z3: _diff_attn_prompt(z3, *lam_args), mk, mv)
        outs["pk"].append(k_new); outs["pv"].append(v_new)
        outs["pmk"].append(mk.reshape(bp, N_MEM, H_MEM, DH_MEM)); outs["pmv"].append(mv.reshape(bp, N_MEM, H_MEM, DH_MEM))
        outs["pconv"].append(cbuf); outs["pshift"].append(hlast); outs["pwkv"].append(wkv)

        z_prev = _matmul(state_shift[l], wl[:, off_rwkv:off_diff].astype(BF16), tn_pref=W_RWKV_IN).reshape(bs, 1, W_RWKV_IN)
        wkv0 = state_wkv[l].transpose(3, 2, 0, 1).reshape(N_RWKV, N_RWKV, bs * H_RWKV)

        def attend_sample(z3):
            pad = lambda a: jnp.pad(a, ((0, 0), (0, PAGE_SIZE - ts), (0, 0)))
            return _diff_attn_sample(_sample_queries(z3), pad(z3[:, :, COL_DK:COL_DK + D_MODEL]),
                                     pad(z3[:, :, COL_DV:COL_DV + D_MODEL]), cache_k4, cache_v4, page_table, l,
                                     *lam_args, n_new=ts)

        xs, cbuf, hlast, wkv, k_new, v_new, _ = _layer(
            xs, lw, z_prev, state_conv[l], wkv0, attend_sample,
            cache_mem_k[l].reshape(bs, N_MEM, D_MODEL), cache_mem_v[l].reshape(bs, N_MEM, D_MODEL))
        outs["sk"].append(k_new); outs["sv"].append(v_new); outs["sconv"].append(cbuf)
        outs["sshift"].append(hlast); outs["swkv"].append(wkv)

    y_prompt = _rmsnorm(xp.reshape(bp * tp, D_MODEL), norm_final).reshape(bp, tp, D_MODEL)
    y_sample = _rmsnorm(xs.reshape(bs * ts, D_MODEL), norm_final).reshape(bs, ts, D_MODEL)
    st = lambda name: jnp.stack(outs[name])
    return (y_prompt, y_sample, st("pk"), st("pv"), st("pmk"), st("pmv"), st("pconv"), st("pshift"), st("pwkv"),
            st("sk"), st("sv"), st("sconv"), st("sshift"), st("swkv"))
```

```python
import functools
import math

import jax
import jax.numpy as jnp
from jax import lax
from jax.experimental import pallas as pl
from jax.experimental.pallas import tpu as pltpu

F32 = jnp.float32
BF16 = jnp.bfloat16

LANES = 128
SUBLANES = 8
VMEM_LIMIT_BYTES = 56 * 1024 * 1024

D_MODEL = 1024
CONV_WIDTH = 31
CONV_HALO = CONV_WIDTH - 1
CONV_PAD = 32
N_RWKV = 64
H_RWKV = D_MODEL // N_RWKV
LORA_DECAY = 64
LORA_ICLR = 64
LORA_GATE = 128
LORA_ALL = LORA_DECAY + LORA_ICLR + LORA_GATE
RWKV_GN_EPS = 64e-5
DK_DIFF = 64
DV_DIFF = 128
H_DIFF = D_MODEL // DV_DIFF
DIFF_SCALE = DK_DIFF ** -0.5
N_MEM = 256
H_MEM = 4
DH_MEM = D_MODEL // H_MEM
N_KEYS = 128
H_PEER = 8
D_HALF = 128
TOPK = 16
TOPK_SHIFT = TOPK.bit_length() - 1
PAGE_SIZE = 128
PAGES_PER_STEP = 8
EPS = 1e-6

COL_CONV_A, COL_CONV_B, COL_R, COL_K, COL_V = 0, 1024, 2048, 3072, 4096
COL_DQ, COL_DK, COL_DV = 5120, 6144, 7168
COL_GATE = 8192
COL_LORA = 11264
D_IN_TOTAL = 11520
W_RWKV_IN = 3 * D_MODEL + LORA_ALL


def _params(*sem):
    return pltpu.CompilerParams(dimension_semantics=sem, vmem_limit_bytes=VMEM_LIMIT_BYTES)


def _tile(n, pref):
    if n <= pref:
        return n
    t = pref
    while n % t:
        t //= 2
    return t


def _rms(x, g, eps):
    return x * lax.rsqrt(jnp.mean(x * x, axis=-1, keepdims=True) + eps) * g


def _dot(a, b):
    return jnp.dot(a, b, preferred_element_type=F32)


def _dot_nt(a, b):
    return lax.dot_general(a, b, (((1,), (1,)), ((), ())), preferred_element_type=F32)


def _mm_kernel(*refs, has_norm):
    if has_norm:
        x_ref, g_ref, w_ref, o_ref, xs_ref = refs
    else:
        x_ref, w_ref, o_ref, xs_ref = refs

    @pl.when(pl.program_id(1) == 0)
    def _():
        x = x_ref[...]
        if has_norm:
            x = _rms(x, g_ref[...], EPS)
        xs_ref[...] = x.astype(BF16)

    o_ref[...] = _dot(xs_ref[...], w_ref[...])


def _matmul(x, w_bf16, norm_g=None, tn_pref=1280):
    m, k = x.shape
    n = w_bf16.shape[1]
    tm = _tile(m, 512)
    tn = n
    for cand in (tn_pref, 1664, 1024, 512, 256, 128):
        if n % cand == 0:
            tn = cand
            break
    in_specs = [pl.BlockSpec((tm, k), lambda i, j: (i, 0))]
    args = [x]
    if norm_g is not None:
        in_specs.append(pl.BlockSpec((1, k), lambda i, j: (0, 0)))
        args.append(norm_g.reshape(1, k))
    in_specs.append(pl.BlockSpec((k, tn), lambda i, j: (0, j)))
    args.append(w_bf16)
    return pl.pallas_call(
        functools.partial(_mm_kernel, has_norm=norm_g is not None),
        grid=(m // tm, n // tn),
        in_specs=in_specs,
        out_specs=pl.BlockSpec((tm, tn), lambda i, j: (i, j)),
        out_shape=jax.ShapeDtypeStruct((m, n), F32),
        scratch_shapes=[pltpu.VMEM((tm, k), BF16)],
        compiler_params=_params("parallel", "arbitrary"),
        name="matmul",
    )(*args)


def _rmsnorm_kernel(x_ref, g_ref, o_ref):
    o_ref[...] = _rms(x_ref[...], g_ref[...], EPS)


def _rmsnorm(x, g):
    m, k = x.shape
    tm = _tile(m, 1024)
    return pl.pallas_call(
        _rmsnorm_kernel,
        grid=(m // tm,),
        in_specs=[pl.BlockSpec((tm, k), lambda i: (i, 0)), pl.BlockSpec((1, k), lambda i: (0, 0))],
        out_specs=pl.BlockSpec((tm, k), lambda i: (i, 0)),
        out_shape=jax.ShapeDtypeStruct((m, k), F32),
        compiler_params=_params("parallel"),
        name="rmsnorm",
    )(x, g.reshape(1, k))


def _conv_kernel(a_ref, b_ref, st_ref, w_ref, cb_ref, g_ref, be_ref, act_ref, ns_ref, ext_ref, *, tt, sub):
    @pl.when(pl.program_id(1) == 0)
    def _():
        ext_ref[0:CONV_PAD - CONV_HALO, :] = jnp.zeros((CONV_PAD - CONV_HALO, D_MODEL), F32)
        ext_ref[CONV_PAD - CONV_HALO:CONV_PAD, :] = st_ref[0]

    ext_ref[CONV_PAD:CONV_PAD + tt, :] = a_ref[0] * jax.nn.sigmoid(b_ref[0])
    base = CONV_PAD - CONV_HALO
    for s in range(tt // sub):
        acc = jnp.zeros((sub, D_MODEL), F32)
        for j in range(CONV_WIDTH):
            lo = base + s * sub + j
            acc = acc + ext_ref[lo:lo + sub, :] * w_ref[j:j + 1, :]
        y = acc + cb_ref[...]
        mu = jnp.mean(y, axis=-1, keepdims=True)
        yc = y - mu
        var = jnp.mean(yc * yc, axis=-1, keepdims=True)
        y = yc * lax.rsqrt(var + 1e-5) * g_ref[...] + be_ref[...]
        act_ref[0, s * sub:(s + 1) * sub, :] = y * jax.nn.sigmoid(y)
    tail = ext_ref[tt + base:tt + CONV_PAD, :]
    ns_ref[0] = tail
    ext_ref[base:CONV_PAD, :] = tail


def _conv_branch(z3, conv_state, conv_w, conv_b, ln_g, ln_b):
    b, t, _ = z3.shape
    tt = _tile(t, 128)
    sub = min(tt, 32)
    row = lambda v: v.reshape(1, D_MODEL)
    cspec = pl.BlockSpec((1, D_MODEL), lambda i, j: (0, 0))
    return pl.pallas_call(
        functools.partial(_conv_kernel, tt=tt, sub=sub),
        grid=(b, t // tt),
        in_specs=[
            pl.BlockSpec((1, tt, D_MODEL), lambda i, j: (i, j, COL_CONV_A // D_MODEL)),
            pl.BlockSpec((1, tt, D_MODEL), lambda i, j: (i, j, COL_CONV_B // D_MODEL)),
            pl.BlockSpec((1, CONV_HALO, D_MODEL), lambda i, j: (i, 0, 0)),
            pl.BlockSpec((CONV_WIDTH, D_MODEL), lambda i, j: (0, 0)),
            cspec, cspec, cspec,
        ],
        out_specs=[
            pl.BlockSpec((1, tt, D_MODEL), lambda i, j: (i, j, 0)),
            pl.BlockSpec((1, CONV_HALO, D_MODEL), lambda i, j: (i, 0, 0)),
        ],
        out_shape=[jax.ShapeDtypeStruct((b, t, D_MODEL), F32), jax.ShapeDtypeStruct((b, CONV_HALO, D_MODEL), F32)],
        scratch_shapes=[pltpu.VMEM((CONV_PAD + tt, D_MODEL), F32)],
        compiler_params=_params("parallel", "arbitrary"),
        name="conv_branch",
    )(z3, z3, conv_state, conv_w, row(conv_b), row(ln_g), row(ln_b))


def _rwkv_prep_kernel(zr_ref, zk_ref, zv_ref, zl_ref, pr_ref, pk_ref, pv_ref, pl_ref,
                      mur_ref, muk_ref, muv_ref, mul_ref, w0_ref, a0_ref, kkp_ref, ka_ref,
                      w2_ref, a2_ref, g2_ref, ones_ref,
                      r_out, w_out, k_out, v_out, kk_out, b_out, g_out,
                      cr_ref, ck_ref, cv_ref, cl_ref):
    @pl.when(pl.program_id(1) == 0)
    def _():
        cr_ref[...] = pr_ref[0]
        ck_ref[...] = pk_ref[0]
        cv_ref[...] = pv_ref[0]
        cl_ref[...] = pl_ref[0]

    def shift_mix(z_ref, carry_ref, mu_ref):
        z = z_ref[0]
        rows = lax.broadcasted_iota(jnp.int32, z.shape, 0)
        prev = jnp.where(rows == 0, carry_ref[...], pltpu.roll(z, 1, axis=0))
        carry_ref[...] = z[z.shape[0] - 1:, :]
        return z + (prev - z) * mu_ref[...]

    r = shift_mix(zr_ref, cr_ref, mur_ref)
    k = shift_mix(zk_ref, ck_ref, muk_ref)
    v = shift_mix(zv_ref, cv_ref, muv_ref)
    lo = shift_mix(zl_ref, cl_ref, mul_ref)

    wl = w0_ref[...] + _dot(jnp.tanh(lo).astype(BF16), w2_ref[...])
    nw = -wl
    w = -(jnp.maximum(nw, 0.0) + jnp.log(1.0 + jnp.exp(-jnp.abs(nw)))) - 0.5
    decay = jnp.exp(-jnp.exp(w))
    a = jax.nn.sigmoid(a0_ref[...] + _dot(lo.astype(BF16), a2_ref[...]))
    g = _dot(jax.nn.sigmoid(lo).astype(BF16), g2_ref[...])
    kkv = k * kkp_ref[...]
    sq = kkv * kkv
    sq_hi = sq.astype(BF16)
    sq_lo = (sq - sq_hi.astype(F32)).astype(BF16)
    ss = _dot(sq_hi, ones_ref[...]) + _dot(sq_lo, ones_ref[...])
    kk = kkv * lax.rsqrt(jnp.maximum(ss, 1e-24))
    r_out[0] = r
    w_out[0] = decay
    k_out[0] = k * (1.0 + (a - 1.0) * ka_ref[...])
    v_out[0] = v
    kk_out[0] = kk
    b_out[0] = kk * a
    g_out[0] = g


def _rwkv_prep(z3, z_prev, mu, w0, w2p, a0, a2p, g2p, kkp, ka, ones_bd):
    b, t, _ = z3.shape
    tt = _tile(t, 256)
    wide = lambda c: pl.BlockSpec((1, tt, D_MODEL), lambda i, j: (i, j, c // D_MODEL))
    prev = lambda c: pl.BlockSpec((1, 1, D_MODEL), lambda i, j: (i, 0, c // D_MODEL))
    vec = pl.BlockSpec((1, D_MODEL), lambda i, j: (0, 0))
    vecl = pl.BlockSpec((1, LORA_ALL), lambda i, j: (0, 0))
    lora_w = pl.BlockSpec((LORA_ALL, D_MODEL), lambda i, j: (0, 0))
    row = lambda v_: v_.reshape(1, -1)
    out = jax.ShapeDtypeStruct((b, t, D_MODEL), F32)
    return pl.pallas_call(
        _rwkv_prep_kernel,
        grid=(b, t // tt),
        in_specs=[
            wide(COL_R), wide(COL_K), wide(COL_V),
            pl.BlockSpec((1, tt, LORA_ALL), lambda i, j: (i, j, COL_LORA // LORA_ALL)),
            prev(0), prev(D_MODEL), prev(2 * D_MODEL),
            pl.BlockSpec((1, 1, LORA_ALL), lambda i, j: (i, 0, 3 * D_MODEL // LORA_ALL)),
            vec, vec, vec, vecl, vec, vec, vec, vec,
            lora_w, lora_w, lora_w,
            pl.BlockSpec((D_MODEL, D_MODEL), lambda i, j: (0, 0)),
        ],
        out_specs=[pl.BlockSpec((1, tt, D_MODEL), lambda i, j: (i, j, 0))] * 7,
        out_shape=[out] * 7,
        scratch_shapes=[pltpu.VMEM((1, D_MODEL), F32)] * 3 + [pltpu.VMEM((1, LORA_ALL), F32)],
        compiler_params=_params("parallel", "arbitrary"),
        name="rwkv_prep",
    )(z3, z3, z3, z3, z_prev, z_prev, z_prev, z_prev,
      row(mu[:D_MODEL]), row(mu[D_MODEL:2 * D_MODEL]), row(mu[2 * D_MODEL:3 * D_MODEL]), row(mu[3 * D_MODEL:]),
      row(w0), row(a0), row(kkp), row(ka), w2p, a2p, g2p, ones_bd)


def _rwkv_scan_kernel(r_ref, w_ref, k_ref, v_ref, kk_ref, b_ref, s0_ref, rk_ref, gg_ref, gb_ref,
                      y_ref, sf_ref, s_ref, *, tc):
    @pl.when(pl.program_id(1) == 0)
    def _():
        s_ref[...] = s0_ref[...]

    def step(t, carry):
        sa = jnp.zeros((N_RWKV, LANES), F32)
        for kx in range(N_RWKV):
            sa = sa + s_ref[kx] * kk_ref[t, kx:kx + 1, :]
        vt = v_ref[t]
        y = jnp.zeros((N_RWKV, LANES), F32)
        for kx in range(N_RWKV):
            s_new = (s_ref[kx] * w_ref[t, kx:kx + 1, :] - sa * b_ref[t, kx:kx + 1, :]
                     + vt * k_ref[t, kx:kx + 1, :])
            s_ref[kx] = s_new
            y = y + s_new * r_ref[t, kx:kx + 1, :]
        bonus = jnp.sum(r_ref[t] * k_ref[t] * rk_ref[...], axis=0, keepdims=True)
        mu = jnp.mean(y, axis=0, keepdims=True)
        yc = y - mu
        var = jnp.mean(yc * yc, axis=0, keepdims=True)
        y_ref[t] = yc * lax.rsqrt(var + RWKV_GN_EPS) * gg_ref[...] + gb_ref[...] + bonus * vt
        return carry

    lax.fori_loop(0, tc, step, 0)

    @pl.when(pl.program_id(1) == pl.num_programs(1) - 1)
    def _():
        sf_ref[...] = s_ref[...]


def _rwkv_scan(r, w, k, v, kk, b, s0, rk, gg, gb):
    t, n, l = r.shape
    tc = _tile(t, 32)
    seq = pl.BlockSpec((tc, n, LANES), lambda i, j: (j, 0, i))
    cst = pl.BlockSpec((n, LANES), lambda i, j: (0, i))
    st = pl.BlockSpec((n, n, LANES), lambda i, j: (0, 0, i))
    return pl.pallas_call(
        functools.partial(_rwkv_scan_kernel, tc=tc),
        grid=(l // LANES, t // tc),
        in_specs=[seq] * 6 + [st, cst, cst, cst],
        out_specs=[seq, st],
        out_shape=[jax.ShapeDtypeStruct((t, n, l), F32), jax.ShapeDtypeStruct((n, n, l), F32)],
        scratch_shapes=[pltpu.VMEM((n, n, LANES), F32)],
        compiler_params=_params("parallel", "arbitrary"),
        name="rwkv_scan",
    )(r, w, k, v, kk, b, s0, rk, gg, gb)


def _diff_lambda(lq1_ref, lk1_ref, lq2_ref, lk2_ref, lam_init):
    return (jnp.exp(jnp.sum(lq1_ref[...] * lk1_ref[...], axis=-1, keepdims=True))
            - jnp.exp(jnp.sum(lq2_ref[...] * lk2_ref[...], axis=-1, keepdims=True)) + lam_init)


def _subln(o, g_ref, lam_init):
    return o * lax.rsqrt(jnp.mean(o * o, axis=-1, keepdims=True) + 1e-5) * g_ref[...] * (1.0 - lam_init)


def _flash_kernel(q_ref, k_ref, v_ref, g_ref, lq1_ref, lk1_ref, lq2_ref, lk2_ref, o_ref,
                  m_ref, l_ref, acc_ref, *, tq, tk, lam_init):
    qi, ki = pl.program_id(2), pl.program_id(3)

    @pl.when(ki == 0)
    def _():
        m_ref[...] = jnp.full(m_ref.shape, -jnp.inf, F32)
        l_ref[...] = jnp.zeros(l_ref.shape, F32)
        acc_ref[...] = jnp.zeros(acc_ref.shape, F32)

    def tile(masked):
        q = q_ref[0] * DIFF_SCALE
        kb = k_ref[0].astype(BF16)
        vb = v_ref[0].astype(BF16)
        lane = lax.broadcasted_iota(jnp.int32, q.shape, 1)
        if masked:
            qpos = qi * tq + lax.broadcasted_iota(jnp.int32, (tq, tk), 0)
            kpos = ki * tk + lax.broadcasted_iota(jnp.int32, (tq, tk), 1)
            mask = kpos <= qpos
        for mi in range(2):
            qm = jnp.where((lane < DK_DIFF) if mi == 0 else (lane >= DK_DIFF), q, 0.0).astype(BF16)
            s = _dot_nt(qm, kb)
            if masked:
                s = jnp.where(mask, s, -jnp.inf)
            m_prev = m_ref[mi]
            m_new = jnp.maximum(m_prev, jnp.max(s, axis=-1, keepdims=True))
            p = jnp.exp(s - m_new)
            alpha = jnp.exp(m_prev - m_new)
            l_ref[mi] = alpha * l_ref[mi] + jnp.sum(p, axis=-1, keepdims=True)
            acc_ref[mi] = alpha * acc_ref[mi] + _dot(p.astype(BF16), vb)
            m_ref[mi] = m_new

    first_key, last_key = ki * tk, ki * tk + tk - 1
    first_query, last_query = qi * tq, qi * tq + tq - 1

    @pl.when(last_key <= first_query)
    def _():
        tile(masked=False)

    @pl.when((last_key > first_query) & (first_key <= last_query))
    def _():
        tile(masked=True)

    @pl.when(ki == pl.num_programs(3) - 1)
    def _():
        lam = _diff_lambda(lq1_ref, lk1_ref, lq2_ref, lk2_ref, lam_init)
        o = acc_ref[0] / l_ref[0] - lam * (acc_ref[1] / l_ref[1])
        o_ref[0] = _subln(o, g_ref, lam_init)


def _diff_attn_prompt(z3, subln_g, lq1, lk1, lq2, lk2, lam_init):
    b, t, _ = z3.shape
    tq = tk = _tile(t, 512)
    qblk, kblk, vblk = COL_DQ // DV_DIFF, COL_DK // DV_DIFF, COL_DV // DV_DIFF
    last_k = lambda qi, ki: jnp.minimum(ki, (qi * tq + tq - 1) // tk)
    small = pl.BlockSpec((1, DK_DIFF), lambda bi, h, qi, ki: (0, 0))
    row = lambda v_: v_.reshape(1, -1)
    return pl.pallas_call(
        functools.partial(_flash_kernel, tq=tq, tk=tk, lam_init=lam_init),
        grid=(b, H_DIFF, t // tq, t // tk),
        in_specs=[
            pl.BlockSpec((1, tq, DV_DIFF), lambda bi, h, qi, ki: (bi, qi, qblk + h)),
            pl.BlockSpec((1, tk, DV_DIFF), lambda bi, h, qi, ki: (bi, last_k(qi, ki), kblk + h)),
            pl.BlockSpec((1, tk, DV_DIFF), lambda bi, h, qi, ki: (bi, last_k(qi, ki), vblk + h)),
            pl.BlockSpec((1, DV_DIFF), lambda bi, h, qi, ki: (0, 0)),
            small, small, small, small,
        ],
        out_specs=pl.BlockSpec((1, tq, DV_DIFF), lambda bi, h, qi, ki: (bi, qi, h)),
        out_shape=jax.ShapeDtypeStruct((b, t, D_MODEL), F32),
        scratch_shapes=[pltpu.VMEM((2, tq, 1), F32), pltpu.VMEM((2, tq, 1), F32), pltpu.VMEM((2, tq, DV_DIFF), F32)],
        compiler_params=_params("parallel", "parallel", "parallel", "arbitrary"),
        name="diff_attn_prompt",
    )(z3, z3, z3, row(subln_g), row(lq1), row(lk1), row(lq2), row(lk2))


def _sample_attn_kernel(pt_ref, wt_ref, *refs, n_new, lam_init):
    kp = refs[:PAGES_PER_STEP]
    vp = refs[PAGES_PER_STEP:2 * PAGES_PER_STEP]
    kn_ref, vn_ref, g_ref, lq1_ref, lk1_ref, lq2_ref, lk2_ref, o_ref, m_ref, l_ref, acc_ref = refs[2 * PAGES_PER_STEP:]
    c = pl.program_id(1)
    rows = H_DIFF * 2 * n_new

    @pl.when(c == 0)
    def _():
        m_ref[...] = jnp.full(m_ref.shape, -jnp.inf, F32)
        l_ref[...] = jnp.zeros(l_ref.shape, F32)
        acc_ref[...] = jnp.zeros(acc_ref.shape, F32)

    wt = wt_ref[0].astype(BF16)

    def online(s, values):
        m_prev = m_ref[...]
        m_new = jnp.maximum(m_prev, jnp.max(s, axis=-1, keepdims=True))
        p = jnp.exp(s - m_new)
        alpha = jnp.exp(m_prev - m_new)
        l_ref[...] = alpha * l_ref[...] + jnp.sum(p, axis=-1, keepdims=True)
        pv = None
        for off, vb in values:
            term = _dot(p[:, off:off + PAGE_SIZE].astype(BF16), vb)
            pv = term if pv is None else pv + term
        acc_ref[...] = alpha * acc_ref[...] + pv
        m_ref[...] = m_new

    def page(ref):
        return jnp.concatenate([ref[0, 0, pl.ds(h, PAGE_SIZE, stride=H_DIFF), :] for h in range(H_DIFF)],
                               axis=1).astype(BF16)

    s = jnp.concatenate([_dot_nt(wt, page(kr)) for kr in kp], axis=1) * DIFF_SCALE
    online(s, [(j * PAGE_SIZE, page(vr)) for j, vr in enumerate(vp)])

    @pl.when(c == pl.num_programs(1) - 1)
    def _():
        sn = _dot_nt(wt, kn_ref[0].astype(BF16)) * DIFF_SCALE
        qidx = lax.broadcasted_iota(jnp.int32, sn.shape, 0) % n_new
        col = lax.broadcasted_iota(jnp.int32, sn.shape, 1)
        online(jnp.where(col <= qidx, sn, -jnp.inf), [(0, vn_ref[0].astype(BF16))])
        lam = _diff_lambda(lq1_ref, lk1_ref, lq2_ref, lk2_ref, lam_init)
        o_all = acc_ref[...] / l_ref[...]
        for h in range(H_DIFF):
            r0 = h * 2 * n_new
            o1 = o_all[r0:r0 + n_new, h * DV_DIFF:(h + 1) * DV_DIFF]
            o2 = o_all[r0 + n_new:r0 + 2 * n_new, h * DV_DIFF:(h + 1) * DV_DIFF]
            o_ref[0, :, h * DV_DIFF:(h + 1) * DV_DIFF] = _subln(o1 - lam * o2, g_ref, lam_init)


def _diff_attn_sample(wt, k_new_pad, v_new_pad, cache_k4, cache_v4, page_table, layer, subln_g,
                      lq1, lk1, lq2, lk2, lam_init, n_new):
    b, rows, _ = wt.shape
    n_pages = page_table.shape[1]
    n_steps = n_pages // PAGES_PER_STEP

    def page_spec(j):
        return pl.BlockSpec((1, 1, PAGE_SIZE * H_DIFF, DV_DIFF),
                            lambda bi, c, pt: (layer, pt[bi * n_pages + c * PAGES_PER_STEP + j], 0, 0))

    small = pl.BlockSpec((1, DK_DIFF), lambda bi, c, pt: (0, 0))
    new_spec = pl.BlockSpec((1, PAGE_SIZE, D_MODEL), lambda bi, c, pt: (bi, 0, 0))
    row = lambda v_: v_.reshape(1, -1)
    return pl.pallas_call(
        functools.partial(_sample_attn_kernel, n_new=n_new, lam_init=lam_init),
        grid_spec=pltpu.PrefetchScalarGridSpec(
            num_scalar_prefetch=1,
            grid=(b, n_steps),
            in_specs=[pl.BlockSpec((1, rows, D_MODEL), lambda bi, c, pt: (bi, 0, 0))]
            + [page_spec(j) for j in range(PAGES_PER_STEP)] * 2
            + [new_spec, new_spec, pl.BlockSpec((1, DV_DIFF), lambda bi, c, pt: (0, 0)), small, small, small, small],
            out_specs=pl.BlockSpec((1, n_new, D_MODEL), lambda bi, c, pt: (bi, 0, 0)),
            scratch_shapes=[pltpu.VMEM((rows, 1), F32), pltpu.VMEM((rows, 1), F32), pltpu.VMEM((rows, D_MODEL), F32)],
        ),
        out_shape=jax.ShapeDtypeStruct((b, n_new, D_MODEL), F32),
        compiler_params=_params("parallel", "arbitrary"),
        name="diff_attn_sample",
    )(page_table.reshape(-1), wt, *([cache_k4] * PAGES_PER_STEP), *([cache_v4] * PAGES_PER_STEP),
      k_new_pad, v_new_pad, row(subln_g), row(lq1), row(lk1), row(lq2), row(lk2))


def _mix_out_kernel(x_ref, ca_ref, ys_ref, gr_ref, da_ref, g0_ref, g1_ref, g2_ref,
                    wc_ref, wr_ref, wd_ref, wo_ref, o_ref):
    yc = _dot(ca_ref[...].astype(BF16), wc_ref[...])
    yr = _dot((ys_ref[...] * gr_ref[...]).astype(BF16), wr_ref[...])
    yd = _dot(da_ref[...].astype(BF16), wd_ref[...])
    merged = (jax.nn.sigmoid(g0_ref[...]) * yc + jax.nn.sigmoid(g1_ref[...]) * yr
              + jax.nn.sigmoid(g2_ref[...]) * yd)
    o_ref[...] = x_ref[...] + _dot(merged.astype(BF16), wo_ref[...])


def _mix_out(x2, conv_act, y_scan, g_rwkv, diff_act, z2, wc, wr, wd, wo):
    m = x2.shape[0]
    tm = _tile(m, 256)
    rowblk = pl.BlockSpec((tm, D_MODEL), lambda i: (i, 0))
    gate = lambda n: pl.BlockSpec((tm, D_MODEL), lambda i: (i, COL_GATE // D_MODEL + n))
    wspec = pl.BlockSpec((D_MODEL, D_MODEL), lambda i: (0, 0))
    return pl.pallas_call(
        _mix_out_kernel,
        grid=(m // tm,),
        in_specs=[rowblk] * 5 + [gate(0), gate(1), gate(2)] + [wspec] * 4,
        out_specs=rowblk,
        out_shape=jax.ShapeDtypeStruct((m, D_MODEL), F32),
        compiler_params=_params("parallel"),
        name="mix_out",
    )(x2, conv_act, y_scan, g_rwkv, diff_act, z2, z2, z2, wc, wr, wd, wo)


def _cross_kernel(x_ref, mk_ref, mv_ref, g_ref, wq_ref, wo_ref, o_ref):
    x = x_ref[0]
    q = _dot(_rms(x, g_ref[...], EPS).astype(BF16), wq_ref[...])
    mk = mk_ref[0].astype(BF16)
    mv = mv_ref[0].astype(BF16)
    heads = []
    for h in range(H_MEM):
        sl = slice(h * DH_MEM, (h + 1) * DH_MEM)
        s = _dot_nt(q[:, sl].astype(BF16), mk[:, sl]) * (DH_MEM ** -0.5)
        p = jnp.exp(s - jnp.max(s, axis=-1, keepdims=True))
        p = p / jnp.sum(p, axis=-1, keepdims=True)
        heads.append(_dot(p.astype(BF16), mv[:, sl]))
    o = jnp.concatenate(heads, axis=1)
    o_ref[0] = x + _dot(o.astype(BF16), wo_ref[...])


def _cross_attn(x3, mk, mv, norm_g, wq, wo):
    b, t, _ = x3.shape
    tm = _tile(t, 512)
    wspec = pl.BlockSpec((D_MODEL, D_MODEL), lambda i, j: (0, 0))
    mem = pl.BlockSpec((1, N_MEM, D_MODEL), lambda i, j: (i, 0, 0))
    xs = pl.BlockSpec((1, tm, D_MODEL), lambda i, j: (i, j, 0))
    return pl.pallas_call(
        _cross_kernel,
        grid=(b, t // tm),
        in_specs=[xs, mem, mem, pl.BlockSpec((1, D_MODEL), lambda i, j: (0, 0)), wspec, wspec],
        out_specs=xs,
        out_shape=jax.ShapeDtypeStruct((b, t, D_MODEL), F32),
        compiler_params=_params("parallel", "parallel"),
        name="cross_attn",
    )(x3, mk, mv, norm_g.reshape(1, D_MODEL), wq, wo)


def _top16(x, n_rows):
    rows = lax.broadcasted_iota(jnp.int32, x.shape, 0)
    rank = lax.broadcasted_iota(jnp.int32, (TOPK, x.shape[1]), 0)
    vals = jnp.zeros((TOPK, x.shape[1]), F32)
    idxs = jnp.zeros((TOPK, x.shape[1]), jnp.int32)
    for it in range(TOPK):
        mx = jnp.max(x, axis=0, keepdims=True)
        sel = jnp.min(jnp.where(x == mx, rows, n_rows), axis=0, keepdims=True)
        vals = jnp.where(rank == it, mx, vals)
        idxs = jnp.where(rank == it, sel, idxs)
        x = jnp.where(rows == sel, -jnp.inf, x)
    return vals, idxs


def _pick(table, idx):
    out = jnp.zeros(idx.shape, jnp.int32)
    for a in range(TOPK):
        out = out + jnp.where(idx == a, table[a:a + 1, :], 0)
    return out


def _peer_route_kernel(x_ref, g_ref, wq_ref, k1_ref, k2_ref, e_ref, gate_ref):
    h = _rms(x_ref[...], g_ref[...], EPS)
    q = _dot(h.astype(BF16), wq_ref[...])
    k1 = k1_ref[...].astype(BF16)
    k2 = k2_ref[...].astype(BF16)
    for hd in range(H_PEER):
        q1 = q[:, (2 * hd) * D_HALF:(2 * hd + 1) * D_HALF].astype(BF16)
        q2 = q[:, (2 * hd + 1) * D_HALF:(2 * hd + 2) * D_HALF].astype(BF16)
        v1, i1 = _top16(_dot_nt(k1, q1), N_KEYS)
        v2, i2 = _top16(_dot_nt(k2, q2), N_KEYS)
        cand = jnp.concatenate([v1[a:a + 1, :] + v2 for a in range(TOPK)], axis=0)
        sc, ci = _top16(cand, TOPK * TOPK)
        e_ref[hd] = _pick(i1, ci >> TOPK_SHIFT) * N_KEYS + _pick(i2, ci & (TOPK - 1))
        p = jnp.exp(sc - sc[0:1, :])
        gate_ref[hd] = p / jnp.sum(p, axis=0, keepdims=True)


def _peer_route(x2, norm_g, wq, k1, k2):
    m = x2.shape[0]
    tm = _tile(m, 256)
    out_spec = pl.BlockSpec((H_PEER, TOPK, tm), lambda i: (0, 0, i))
    return pl.pallas_call(
        _peer_route_kernel,
        grid=(m // tm,),
        in_specs=[
            pl.BlockSpec((tm, D_MODEL), lambda i: (i, 0)),
            pl.BlockSpec((1, D_MODEL), lambda i: (0, 0)),
            pl.BlockSpec((D_MODEL, H_PEER * 2 * D_HALF), lambda i: (0, 0)),
            pl.BlockSpec((N_KEYS, D_HALF), lambda i: (0, 0)),
            pl.BlockSpec((N_KEYS, D_HALF), lambda i: (0, 0)),
        ],
        out_specs=[out_spec, out_spec],
        out_shape=[jax.ShapeDtypeStruct((H_PEER, TOPK, m), jnp.int32), jax.ShapeDtypeStruct((H_PEER, TOPK, m), F32)],
        compiler_params=_params("parallel"),
        name="peer_route",
    )(x2, norm_g.reshape(1, D_MODEL), wq, k1, k2)


PEER_TOKENS = 8
PEER_PICKS = H_PEER * TOPK
PEER_ROWS = PEER_TOKENS * PEER_PICKS
PEER_ISSUE_UNROLL = 8
VEC_SUB = D_MODEL // LANES


def _peer_gather_copy(tab_ref, buf_ref, sem_ref, slot, expert, row):
    return pltpu.make_async_copy(tab_ref.at[pl.ds(expert, 1)], buf_ref.at[slot, pl.ds(row, 1)], sem_ref.at[slot])


def _peer_expert_kernel(e_cur_ref, e_nxt_ref, gate_ref, x_ref, g_ref, tab_ref, o_ref, buf_ref, sem_ref):
    i = pl.program_id(0)
    n = pl.num_programs(0)
    slot = i % 2

    def issue(e_ref, to_slot):
        def body(c, carry):
            base = pl.multiple_of(c * PEER_ISSUE_UNROLL, PEER_ISSUE_UNROLL)
            for j in range(PEER_ISSUE_UNROLL):
                _peer_gather_copy(tab_ref, buf_ref, sem_ref, to_slot, e_ref[base + j], base + j).start()
            return carry
        lax.fori_loop(0, PEER_ROWS // PEER_ISSUE_UNROLL, body, 0)

    @pl.when(i == 0)
    def _():
        issue(e_cur_ref, 0)

    @pl.when(i + 1 < n)
    def _():
        issue(e_nxt_ref, 1 - slot)

    pltpu.make_async_copy(tab_ref.at[pl.ds(0, PEER_ROWS)], buf_ref.at[slot], sem_ref.at[slot]).wait()

    x = x_ref[...]
    ms = jnp.sum(jnp.sum(x * x, axis=2, keepdims=True), axis=1, keepdims=True) * (1.0 / D_MODEL)
    h = x * lax.rsqrt(ms + EPS) * g_ref[...]
    for t in range(PEER_TOKENS):
        r0 = t * PEER_PICKS
        prod = buf_ref[slot, r0:r0 + PEER_PICKS, :VEC_SUB, :] * h[t]
        hu = jnp.sum(jnp.sum(prod, axis=1, keepdims=True), axis=2, keepdims=True)
        act = 0.5 * hu * (1.0 + lax.erf(hu * (2.0 ** -0.5)))
        accs = [jnp.zeros((VEC_SUB, LANES), F32) for _ in range(4)]
        for p in range(PEER_PICKS):
            wgt = act[p] * gate_ref[r0 + p]
            accs[p % 4] = accs[p % 4] + wgt * buf_ref[slot, r0 + p, VEC_SUB:, :]
        o_ref[t] = x[t] + ((accs[0] + accs[1]) + (accs[2] + accs[3]))


def _peer_experts(x2, norm_g, e_flat, gate_flat, table3):
    m = x2.shape[0]
    n = m // PEER_TOKENS
    tok = pl.BlockSpec((PEER_TOKENS, VEC_SUB, LANES), lambda i: (i, 0, 0))
    return pl.pallas_call(
        _peer_expert_kernel,
        grid=(n,),
        in_specs=[
            pl.BlockSpec((PEER_ROWS,), lambda i: (i,), memory_space=pltpu.SMEM),
            pl.BlockSpec((PEER_ROWS,), lambda i: (jnp.minimum(i + 1, n - 1),), memory_space=pltpu.SMEM),
            pl.BlockSpec((PEER_ROWS,), lambda i: (i,), memory_space=pltpu.SMEM),
            tok,
            pl.BlockSpec((VEC_SUB, LANES), lambda i: (0, 0)),
            pl.BlockSpec(memory_space=pl.ANY),
        ],
        out_specs=tok,
        out_shape=jax.ShapeDtypeStruct((m, VEC_SUB, LANES), F32),
        scratch_shapes=[pltpu.VMEM((2, PEER_ROWS, 2 * VEC_SUB, LANES), F32), pltpu.SemaphoreType.DMA((2,))],
        compiler_params=_params("arbitrary"),
        name="peer_experts",
    )(e_flat, e_flat, gate_flat, x2.reshape(m, VEC_SUB, LANES), norm_g.reshape(VEC_SUB, LANES), table3).reshape(m, D_MODEL)


def _to_lanes(a, b, t):
    return a.reshape(b, t, H_RWKV, N_RWKV).transpose(1, 3, 0, 2).reshape(t, N_RWKV, b * H_RWKV)


def _head_const(v, b):
    return jnp.tile(v.reshape(H_RWKV, N_RWKV).T, (1, b))


def _layer(x3, lw, z_prev, conv_state, wkv0_lanes, attend, mk, mv):
    b, t, _ = x3.shape
    m = b * t
    x2 = x3.reshape(m, D_MODEL)
    z2 = _matmul(x2, lw["w_in"], norm_g=lw["norm_mix"])
    z3 = z2.reshape(b, t, D_IN_TOTAL)
    h_last = _rmsnorm(x3[:, -1, :], lw["norm_mix"])

    conv_act, conv_new = _conv_branch(z3, conv_state, lw["conv_w"], lw["conv_b"], lw["conv_ln_g"], lw["conv_ln_b"])

    r, w, k, v, kk, bb, g_rwkv = _rwkv_prep(z3, z_prev, lw["rwkv_mu"], lw["rwkv_w0"], lw["w2p"], lw["rwkv_a0"],
                                            lw["a2p"], lw["g2p"], lw["rwkv_kk"], lw["rwkv_ka"], lw["ones_bd"])
    lanes = lambda a: _to_lanes(a, b, t)
    y_l, s_fin = _rwkv_scan(lanes(r), lanes(w), lanes(k), lanes(v), lanes(kk), lanes(bb), wkv0_lanes,
                            _head_const(lw["rwkv_rk"].reshape(-1), b), _head_const(lw["rwkv_gn_g"], b),
                            _head_const(lw["rwkv_gn_b"], b))
    y_scan = y_l.reshape(t, N_RWKV, b, H_RWKV).transpose(2, 0, 3, 1).reshape(m, D_MODEL)
    wkv_new = s_fin.reshape(N_RWKV, N_RWKV, b, H_RWKV).transpose(2, 3, 1, 0)

    diff_act = attend(z3)
    x2 = _mix_out(x2, conv_act.reshape(m, D_MODEL), y_scan, g_rwkv.reshape(m, D_MODEL), diff_act.reshape(m, D_MODEL),
                  z2, lw["w_conv_out"], lw["w_rwkv_out"], lw["w_attn_out"], lw["w_o"])

    x3 = _cross_attn(x2.reshape(b, t, D_MODEL), mk, mv, lw["norm_cross"], lw["w_cq"], lw["w_co"])
    x2 = x3.reshape(m, D_MODEL)

    e, gate = _peer_route(x2, lw["norm_ffn"], lw["w_pq"], lw["peer_k1"], lw["peer_k2"])
    tok = lambda a: a.reshape(PEER_PICKS, m).T.reshape(m * PEER_PICKS)
    x2 = _peer_experts(x2, lw["norm_ffn"], tok(e), tok(gate), lw["peer_uv"])

    k_new = z3[:, :, COL_DK:COL_DK + D_MODEL].reshape(b, t, H_DIFF, DV_DIFF)
    v_new = z3[:, :, COL_DV:COL_DV + D_MODEL].reshape(b, t, H_DIFF, DV_DIFF)
    return x2.reshape(b, t, D_MODEL), conv_new, h_last, wkv_new, k_new, v_new, z3


def _sample_queries(z3):
    b, t, _ = z3.shape
    q = z3[:, :, COL_DQ:COL_DQ + D_MODEL].reshape(b, t, 2 * H_DIFF, DK_DIFF).transpose(0, 2, 1, 3)
    eye = jnp.eye(2 * H_DIFF, dtype=F32)
    return jnp.einsum("bxid,xy->bxiyd", q, eye).reshape(b, 2 * H_DIFF * t, D_MODEL)


def kernel(x_prompt, x_sample, mem_prompt, cache_k, cache_v, page_table, cache_mem_k, cache_mem_v, state_conv, state_shift, state_wkv, norm_mix, w_in, conv_w, conv_b, conv_ln_g, conv_ln_b, w_conv_out, rwkv_mu, rwkv_w0, rwkv_w2, rwkv_a0, rwkv_a2, rwkv_g2, rwkv_kk, rwkv_ka, rwkv_rk, rwkv_gn_g, rwkv_gn_b, w_rwkv_out, attn_lq1, attn_lk1, attn_lq2, attn_lk2, attn_subln, w_attn_out, w_o, norm_cross, norm_mem, w_cq, w_mk, w_mv, w_co, norm_ffn, w_pq, peer_k1, peer_k2, peer_u, peer_v, norm_final):
    depth = w_in.shape[0]
    bp, tp, _ = x_prompt.shape
    bs, ts, _ = x_sample.shape
    n_pool = cache_k.shape[1]
    cache_k4 = cache_k.reshape(depth, n_pool, PAGE_SIZE * H_DIFF, DV_DIFF)
    cache_v4 = cache_v.reshape(depth, n_pool, PAGE_SIZE * H_DIFF, DV_DIFF)
    head_ids = jnp.arange(D_MODEL) // N_RWKV
    ones_bd = (head_ids[:, None] == head_ids[None, :]).astype(BF16)
    off_rwkv, off_diff, off_gate = 2 * D_MODEL, 2 * D_MODEL + W_RWKV_IN, 2 * D_MODEL + W_RWKV_IN + 3 * D_MODEL
    lora0 = off_rwkv + 3 * D_MODEL

    xp, xs = x_prompt, x_sample
    outs = {name: [] for name in ("pk", "pv", "pmk", "pmv", "pconv", "pshift", "pwkv", "sk", "sv", "sconv", "sshift", "swkv")}
    for l in range(depth):
        wl = w_in[l]
        zpad = lambda a, lo: jnp.zeros((LORA_ALL, D_MODEL), F32).at[lo:lo + a.shape[0]].set(a).astype(BF16)
        lw = {
            "w_in": jnp.concatenate([wl[:, :lora0], wl[:, off_diff:], wl[:, lora0:off_diff]], axis=1).astype(BF16),
            "norm_mix": norm_mix[l], "conv_w": conv_w[l], "conv_b": conv_b[l], "conv_ln_g": conv_ln_g[l],
            "conv_ln_b": conv_ln_b[l], "rwkv_mu": rwkv_mu[l], "rwkv_w0": rwkv_w0[l], "rwkv_a0": rwkv_a0[l],
            "w2p": zpad(rwkv_w2[l], 0), "a2p": zpad(rwkv_a2[l], LORA_DECAY), "g2p": zpad(rwkv_g2[l], LORA_DECAY + LORA_ICLR),
            "rwkv_kk": rwkv_kk[l], "rwkv_ka": rwkv_ka[l], "rwkv_rk": rwkv_rk[l], "rwkv_gn_g": rwkv_gn_g[l],
            "rwkv_gn_b": rwkv_gn_b[l], "ones_bd": ones_bd,
            "w_conv_out": w_conv_out[l].astype(BF16), "w_rwkv_out": w_rwkv_out[l].astype(BF16),
            "w_attn_out": w_attn_out[l].astype(BF16), "w_o": w_o[l].astype(BF16),
            "norm_cross": norm_cross[l], "w_cq": w_cq[l].astype(BF16), "w_co": w_co[l].astype(BF16),
            "norm_ffn": norm_ffn[l], "w_pq": w_pq[l].astype(BF16), "peer_k1": peer_k1[l], "peer_k2": peer_k2[l],
            "peer_uv": jnp.concatenate([peer_u[l], peer_v[l]], axis=1).reshape(-1, 2 * VEC_SUB, LANES),
        }
        lam_init = 0.8 - 0.6 * math.exp(-0.3 * l)
        lam_args = (attn_subln[l], attn_lq1[l], attn_lk1[l], attn_lq2[l], attn_lk2[l], lam_init)

        hm = _matmul(mem_prompt.reshape(bp * N_MEM, D_MODEL),
                     jnp.concatenate([w_mk[l], w_mv[l]], axis=1).astype(BF16), norm_g=norm_mem[l], tn_pref=1024)
        mk = hm[:, :D_MODEL].reshape(bp, N_MEM, D_MODEL)
        mv = hm[:, D_MODEL:].reshape(bp, N_MEM, D_MODEL)
        xp, cbuf, hlast, wkv, k_new, v_new, _ = _layer(
            xp, lw, jnp.zeros((bp, 1, W_RWKV_IN), F32), jnp.zeros((bp, CONV_HALO, D_MODEL), F32),
            jnp.zeros((N_RWKV, N_RWKV, bp * H_RWKV), F32),
            lambda z3: _diff_attn_prompt(z3, *lam_args), mk, mv)
        outs["pk"].append(k_new); outs["pv"].append(v_new)
        outs["pmk"].append(mk.reshape(bp, N_MEM, H_MEM, DH_MEM)); outs["pmv"].append(mv.reshape(bp, N_MEM, H_MEM, DH_MEM))
        outs["pconv"].append(cbuf); outs["pshift"].append(hlast); outs["pwkv"].append(wkv)

        z_prev = _matmul(state_shift[l], wl[:, off_rwkv:off_diff].astype(BF16), tn_pref=W_RWKV_IN).reshape(bs, 1, W_RWKV_IN)
        wkv0 = state_wkv[l].transpose(3, 2, 0, 1).reshape(N_RWKV, N_RWKV, bs * H_RWKV)

        def attend_sample(z3):
            pad = lambda a: jnp.pad(a, ((0, 0), (0, PAGE_SIZE - ts), (0, 0)))
            return _diff_attn_sample(_sample_queries(z3), pad(z3[:, :, COL_DK:COL_DK + D_MODEL]),
                                     pad(z3[:, :, COL_DV:COL_DV + D_MODEL]), cache_k4, cache_v4, page_table, l,
                                     *lam_args, n_new=ts)

        xs, cbuf, hlast, wkv, k_new, v_new, _ = _layer(
            xs, lw, z_prev, state_conv[l], wkv0, attend_sample,
            cache_mem_k[l].reshape(bs, N_MEM, D_MODEL), cache_mem_v[l].reshape(bs, N_MEM, D_MODEL))
        outs["sk"].append(k_new); outs["sv"].append(v_new); outs["sconv"].append(cbuf)
        outs["sshift"].append(hlast); outs["swkv"].append(wkv)

    y_prompt = _rmsnorm(xp.reshape(bp * tp, D_MODEL), norm_final).reshape(bp, tp, D_MODEL)
    y_sample = _rmsnorm(xs.reshape(bs * ts, D_MODEL), norm_final).reshape(bs, ts, D_MODEL)
    st = lambda name: jnp.stack(outs[name])
    return (y_prompt, y_sample, st("pk"), st("pv"), st("pmk"), st("pmv"), st("pconv"), st("pshift"), st("pwkv"),
            st("sk"), st("sv"), st("sconv"), st("sshift"), st("swkv"))
```
